```python
import math
import jax, jax.numpy as jnp
from jax import lax
import numpy as np

D_MODEL = 1024
BATCH = 8
SEQ = 2048
DEPTH = 4

A_GROUPS = ((128, 1), (512, 4), (2048, 16))
A_HEADS_PER_GROUP = 2
A_HEADS = 6
A_HEAD_DIM = 64
A_WIDTH = A_HEADS * A_HEAD_DIM
B_HEADS = 4
B_HEAD_DIM = 64
B_WIDTH = B_HEADS * 2 * B_HEAD_DIM
C_HEADS = 4
C_KEY_DIM = 64
C_VAL_DIM = 128
C_QK_WIDTH = C_HEADS * C_KEY_DIM
C_WIDTH = C_HEADS * C_VAL_DIM
RET_CHUNK = 128
RET_THETA = 10000.0
ROPE_THETA = 500000.0
ROPE_FRACTION = 4
Q_BLOCK = 128
D_FF = 4 * D_MODEL
N_BRANCHES = 3
NORM_EPS = 1e-6
NEG_INF = -1e30
IN_SPLITS = (A_WIDTH, A_WIDTH, A_WIDTH,
             B_WIDTH, B_WIDTH, B_WIDTH,
             C_QK_WIDTH, C_QK_WIDTH, C_WIDTH, C_WIDTH,
             N_BRANCHES * D_MODEL)
N_IN = sum(IN_SPLITS)

kernel_name = "hybrid_dilated_diff_retention_encoder"


def rms_norm(t, g):
    tf = t.astype(jnp.float32)
    y = tf * lax.rsqrt(jnp.mean(tf * tf, axis=-1, keepdims=True) + NORM_EPS)
    return y.astype(t.dtype) * g


def rope_tables(seq, rot_dims, theta):
    inv = 1.0 / (theta ** (jnp.arange(0, rot_dims, 2, dtype=jnp.float32) / rot_dims))
    ang = jnp.arange(seq, dtype=jnp.float32)[:, None] * inv[None, :]
    return jnp.cos(ang), jnp.sin(ang)


def apply_rope(t, cos, sin):
    r2 = cos.shape[-1]
    shape = (t.shape[1],) + (1,) * (t.ndim - 3) + (r2,)
    c = cos.reshape(shape).astype(t.dtype)
    s = sin.reshape(shape).astype(t.dtype)
    t1, t2, tp = t[..., :r2], t[..., r2:2 * r2], t[..., 2 * r2:]
    return jnp.concatenate([t1 * c - t2 * s, t2 * c + t1 * s, tp], axis=-1)


def dilated_window_attention(q, k, v, dilation, half):
    B, S, H, d = q.shape
    L = S // dilation
    blk = half
    nb = -(-L // blk)
    Lp = nb * blk

    def to_sub(t):
        return t.reshape(B, L, dilation, H, d).transpose(0, 2, 3, 1, 4)

    qs, ks, vs = to_sub(q), to_sub(k), to_sub(v)
    qb = jnp.pad(qs, ((0, 0),) * 3 + ((0, Lp - L), (0, 0))).reshape(B, dilation, H, nb, blk, d)

    def windows(t):
        tb = jnp.pad(t, ((0, 0),) * 3 + ((blk, Lp - L + blk), (0, 0))).reshape(B, dilation, H, nb + 2, blk, d)
        return jnp.concatenate([tb[:, :, :, :-2], tb[:, :, :, 1:-1], tb[:, :, :, 2:]], axis=-2)

    kw, vw = windows(ks), windows(vs)
    qi = jnp.arange(nb)[:, None, None] * blk + jnp.arange(blk)[None, :, None]
    kj = jnp.arange(nb)[:, None, None] * blk - blk + jnp.arange(3 * blk)[None, None, :]
    valid = (jnp.abs(qi - kj) <= half) & (kj >= 0) & (kj < L)

    s = jnp.einsum('brhnqd,brhnkd->brhnqk', qb, kw).astype(jnp.float32) * (d ** -0.5)
    s = jnp.where(valid, s, NEG_INF)
    m = jnp.max(s, axis=-1, keepdims=True)
    e = jnp.exp(s - m)
    den = jnp.sum(e, axis=-1, keepdims=True)
    p = e / den
    lse = (m + jnp.log(den))[..., 0]
    o = jnp.einsum('brhnqk,brhnkd->brhnqd', p.astype(v.dtype), vw)
    o = o.reshape(B, dilation, H, Lp, d)[:, :, :, :L].transpose(0, 3, 1, 2, 4).reshape(B, S, H, d)
    lse = lse.reshape(B, dilation, H, Lp)[..., :L].transpose(0, 3, 1, 2).reshape(B, S, H)
    return o, lse


def dilated_mixer(q, k, v):
    B, S = q.shape[:2]
    outs, lses = [], []
    for g, (window, dilation) in enumerate(A_GROUPS):
        hs = slice(g * A_HEADS_PER_GROUP, (g + 1) * A_HEADS_PER_GROUP)
        o, lse = dilated_window_attention(q[:, :, hs], k[:, :, hs], v[:, :, hs], dilation, window // (2 * dilation))
        outs.append(o)
        lses.append(lse)
    alpha = jax.nn.softmax(jnp.stack(lses, axis=2), axis=2)
    o = jnp.stack(outs, axis=2) * alpha[..., None].astype(v.dtype)
    return o.reshape(B, S, A_WIDTH)


def diff_attention(q, k, v, lam):
    B, S, H, _, d = q.shape
    nb = S // Q_BLOCK
    qb = q.reshape(B, nb, Q_BLOCK, H, 2, d).transpose(1, 0, 2, 3, 4, 5)
    scale = d ** -0.5

    def block(qblk):
        s = jnp.einsum('bqhmd,bkhmd->bhmqk', qblk, k).astype(jnp.float32) * scale
        p = jax.nn.softmax(s, axis=-1)
        w = p[:, :, 0] - lam * p[:, :, 1]
        return jnp.einsum('bhqk,bkhe->bqhe', w.astype(v.dtype), v)

    out = lax.map(block, qb)
    return out.transpose(1, 0, 2, 3, 4).reshape(B, S, H, 2 * d)


def retention_direction(q, k, v, log_gamma):
    B, S, H, dk = q.shape
    dv = v.shape[-1]
    C = RET_CHUNK
    n = S // C
    pos = jnp.arange(C, dtype=jnp.float32)
    lg = log_gamma.astype(jnp.float32)
    rel = pos[:, None] - pos[None, :]
    decay_in = jnp.where(rel >= 0, jnp.exp(lg[:, None, None] * jnp.maximum(rel, 0.0)), 0.0)
    decay_q = jnp.exp(lg[:, None] * (pos + 1.0))[..., None]
    decay_k = jnp.exp(lg[:, None] * (C - 1.0 - pos))[..., None]
    decay_chunk = jnp.exp(lg * C)[:, None, None]

    def chunks(t):
        return t.reshape(B, n, C, H, t.shape[-1]).transpose(1, 0, 3, 2, 4)

    def step(state, inp):
        qc, kc, vc = inp
        scores = jnp.einsum('bhqd,bhkd->bhqk', qc, kc) * decay_in
        out = (jnp.einsum('bhqk,bhkv->bhqv', scores, vc)
               + jnp.einsum('bhqd,bhdv->bhqv', qc, state) * decay_q)
        state = decay_chunk * state + jnp.einsum('bhkd,bhkv->bhdv', kc * decay_k, vc)
        return state, out

    state0 = jnp.zeros((B, H, dk, dv), jnp.float32)
    _, ys = lax.scan(step, state0, (chunks(q), chunks(k), chunks(v)))
    return ys.transpose(1, 0, 3, 2, 4).reshape(B, S, H, dv)


def bidirectional_retention(q, k, v, decay_f, decay_b):
    qf, kf, vf = q.astype(jnp.float32), k.astype(jnp.float32), v.astype(jnp.float32)
    fwd = retention_direction(qf, kf, vf, -jnp.exp(decay_f.astype(jnp.float32)))
    flip = lambda t: jnp.flip(t, axis=1)
    bwd = flip(retention_direction(flip(qf), flip(kf), flip(vf), -jnp.exp(decay_b.astype(jnp.float32))))
    return (fwd + bwd).astype(v.dtype)


def setup_inputs(seed: int = 0) -> dict:
    key = jax.random.key(seed)
    ks = jax.random.split(key, 24)
    f32 = jnp.float32

    def nrm(k, shape, scale):
        return jax.random.normal(k, shape, f32) * scale

    def gain(k, shape):
        return 1.0 + 0.02 * jax.random.normal(k, shape, f32)

    base = jnp.log(-jnp.log1p(-(2.0 ** (-5.0 - jnp.arange(C_HEADS, dtype=f32)))))
    return {
        "x": jax.random.normal(ks[0], (BATCH, SEQ, D_MODEL), f32),
        "norm1_g": gain(ks[1], (DEPTH, D_MODEL)),
        "w_in": nrm(ks[2], (DEPTH, D_MODEL, N_IN), D_MODEL ** -0.5),
        "a_q_norm_g": gain(ks[3], (DEPTH, A_HEAD_DIM)),
        "a_k_norm_g": gain(ks[4], (DEPTH, A_HEAD_DIM)),
        "b_q_norm_g": gain(ks[5], (DEPTH, B_HEAD_DIM)),
        "b_k_norm_g": gain(ks[6], (DEPTH, B_HEAD_DIM)),
        "b_lambda_q1": nrm(ks[7], (DEPTH, B_HEAD_DIM), 0.1),
        "b_lambda_k1": nrm(ks[8], (DEPTH, B_HEAD_DIM), 0.1),
        "b_lambda_q2": nrm(ks[9], (DEPTH, B_HEAD_DIM), 0.1),
        "b_lambda_k2": nrm(ks[10], (DEPTH, B_HEAD_DIM), 0.1),
        "b_out_norm_g": gain(ks[11], (DEPTH, 2 * B_HEAD_DIM)),
        "c_decay_f": base[None, :] + nrm(ks[12], (DEPTH, C_HEADS), 0.05),
        "c_decay_b": base[None, :] + nrm(ks[13], (DEPTH, C_HEADS), 0.05),
        "c_out_norm_g": gain(ks[14], (DEPTH, C_VAL_DIM)),
        "w_br_a": nrm(ks[15], (DEPTH, A_WIDTH, D_MODEL), A_WIDTH ** -0.5),
        "w_br_b": nrm(ks[16], (DEPTH, B_WIDTH, D_MODEL), B_WIDTH ** -0.5),
        "w_br_c": nrm(ks[17], (DEPTH, C_WIDTH, D_MODEL), C_WIDTH ** -0.5),
        "w_o": nrm(ks[18], (DEPTH, D_MODEL, D_MODEL), D_MODEL ** -0.5),
        "norm2_g": gain(ks[19], (DEPTH, D_MODEL)),
        "w_mlp1": nrm(ks[20], (DEPTH, D_MODEL, D_FF), D_MODEL ** -0.5),
        "w_mlp2": nrm(ks[21], (DEPTH, D_FF, D_MODEL), D_FF ** -0.5),
    }


def reference(x, norm1_g, w_in, a_q_norm_g, a_k_norm_g, b_q_norm_g, b_k_norm_g,
              b_lambda_q1, b_lambda_k1, b_lambda_q2, b_lambda_k2, b_out_norm_g,
              c_decay_f, c_decay_b, c_out_norm_g, w_br_a, w_br_b, w_br_c, w_o,
              norm2_g, w_mlp1, w_mlp2):
    B, S, D = x.shape
    cos_p, sin_p = rope_tables(S, A_HEAD_DIM // ROPE_FRACTION, ROPE_THETA)
    cos_r, sin_r = rope_tables(S, C_KEY_DIM, RET_THETA)
    split_pts = [int(p) for p in np.cumsum(IN_SPLITS)[:-1]]

    for l in range(DEPTH):
        xn = rms_norm(x, norm1_g[l])
        proj = xn @ w_in[l]
        aq, ak, av, bq, bk, bv, cq, ck, cv, cg, gates = jnp.split(proj, split_pts, axis=-1)

        aq = apply_rope(rms_norm(aq.reshape(B, S, A_HEADS, A_HEAD_DIM), a_q_norm_g[l]), cos_p, sin_p)
        ak = apply_rope(rms_norm(ak.reshape(B, S, A_HEADS, A_HEAD_DIM), a_k_norm_g[l]), cos_p, sin_p)
        av = av.reshape(B, S, A_HEADS, A_HEAD_DIM)
        ya = dilated_mixer(aq, ak, av)

        lam_init = 0.8 - 0.6 * math.exp(-0.3 * l)
        lam = (jnp.exp(jnp.sum(b_lambda_q1[l] * b_lambda_k1[l]))
               - jnp.exp(jnp.sum(b_lambda_q2[l] * b_lambda_k2[l])) + lam_init)
        bq = apply_rope(rms_norm(bq.reshape(B, S, B_HEADS, 2, B_HEAD_DIM), b_q_norm_g[l]), cos_p, sin_p)
        bk = apply_rope(rms_norm(bk.reshape(B, S, B_HEADS, 2, B_HEAD_DIM), b_k_norm_g[l]), cos_p, sin_p)
        bv = bv.reshape(B, S, B_HEADS, 2 * B_HEAD_DIM)
        yb = diff_attention(bq, bk, bv, lam)
        yb = (rms_norm(yb, b_out_norm_g[l]) * (1.0 - lam_init)).reshape(B, S, B_WIDTH)

        cq = apply_rope(cq.reshape(B, S, C_HEADS, C_KEY_DIM), cos_r, sin_r)
        ck = apply_rope(ck.reshape(B, S, C_HEADS, C_KEY_DIM), cos_r, sin_r) * (C_KEY_DIM ** -0.5)
        cv = cv.reshape(B, S, C_HEADS, C_VAL_DIM)
        yc = bidirectional_retention(cq, ck, cv, c_decay_f[l], c_decay_b[l])
        yc = jax.nn.silu(cg) * rms_norm(yc, c_out_norm_g[l]).reshape(B, S, C_WIDTH)

        g = jax.nn.sigmoid(gates.reshape(B, S, N_BRANCHES, D))
        merged = (g[:, :, 0] * (ya @ w_br_a[l])
                  + g[:, :, 1] * (yb @ w_br_b[l])
                  + g[:, :, 2] * (yc @ w_br_c[l]))
        x = x + merged @ w_o[l]

        h = jnp.square(jax.nn.relu(rms_norm(x, norm2_g[l]) @ w_mlp1[l]))
        x = x + h @ w_mlp2[l]
    return x
```

```python
import functools
import math

import jax
import jax.numpy as jnp
import numpy as np
from jax import lax
from jax.experimental import pallas as pl
from jax.experimental.pallas import tpu as pltpu

F32 = jnp.float32
BF16 = jnp.bfloat16

LANES = 128
MXU_DIM = 256
VMEM_LIMIT_BYTES = 56 * 1024 * 1024

D_MODEL = 1024
HEAD_DIM = 64
A_GROUPS = ((128, 1), (512, 4), (2048, 16))
A_WIDTH = 384
B_HEADS = 4
B_WIDTH = 512
C_HEADS = 4
C_QK_WIDTH = 256
C_WIDTH = 512
D_FF = 4096
N_BRANCHES = 3
ROPE_THETA = 500000.0
RET_THETA = 10000.0
PART_ROT = 16
NORM_EPS = 1e-6
NEG_INF = -1e30

OFF_AQ, OFF_AK, OFF_AV = 0, 384, 768
OFF_BQ, OFF_BK, OFF_BV = 1152, 1664, 2176
OFF_CQ, OFF_CK, OFF_CV, OFF_CG = 2688, 2944, 3200, 3712
N_MAIN = 4224
N_GATE = N_BRANCHES * D_MODEL
N_IN = N_MAIN + N_GATE

IN_TM = 512
OUT_TM = 256
B_TQ = 256
RET_CHUNK = 256
A_TQ = 128
A_HALF = 64


def _params(*sem):
    return pltpu.CompilerParams(dimension_semantics=sem, vmem_limit_bytes=VMEM_LIMIT_BYTES)


def _resident(shape):
    nd = len(shape)
    return pl.BlockSpec(shape, lambda *_: (0,) * nd, pipeline_mode=pl.Buffered(1))


def _rope_tables(seq, rot, theta, head_dim):
    half = rot // 2
    inv = 1.0 / (theta ** (jnp.arange(0, rot, 2, dtype=F32) / rot))
    ang = jnp.arange(seq, dtype=F32)[:, None] * inv[None, :]
    cos, sin = jnp.cos(ang), jnp.sin(ang)
    pad = head_dim - rot
    ones = jnp.ones((seq, pad), F32)
    zeros = jnp.zeros((seq, pad), F32)
    zh = jnp.zeros((seq, half), F32)
    c = jnp.concatenate([cos, cos, ones], axis=1)
    s_up = jnp.concatenate([zh, sin, zeros], axis=1)
    s_dn = jnp.concatenate([-sin, zh, zeros], axis=1)
    reps = LANES // head_dim
    return jnp.stack([jnp.tile(t, (1, reps)) for t in (c, s_up, s_dn)], axis=0)


def _apply_rope(t, tab, half):
    return (t * tab[0]
            + pltpu.roll(t, half, 1) * tab[1]
            + pltpu.roll(t, LANES - half, 1) * tab[2])


def _in_proj_kernel(x_ref, g_ref, w_ref, bd_ref, hg_ref, rp_ref, rr_ref, main_ref, gate_ref):
    x = x_ref[...]
    ms = jnp.mean(x * x, axis=-1, keepdims=True)
    xn = (x * lax.rsqrt(ms + NORM_EPS) * g_ref[...]).astype(BF16)
    bd = bd_ref[...]
    rp = rp_ref[...]
    rr = rr_ref[...]

    def proj(lo, hi):
        return jnp.dot(xn, w_ref[:, lo:hi], preferred_element_type=F32)

    def store_roped(t, lo, tab, half, scale=None):
        for c in range(t.shape[1] // LANES):
            o = _apply_rope(t[:, c * LANES:(c + 1) * LANES], tab, half)
            if scale is not None:
                o = o * scale
            main_ref[:, lo + c * LANES:lo + (c + 1) * LANES] = o.astype(BF16)

    hg_off = 0
    for seg_lo, seg_hi in ((OFF_AQ, OFF_AV), (OFF_BQ, OFF_BV)):
        for lo in range(seg_lo, seg_hi, MXU_DIM):
            t = proj(lo, lo + MXU_DIM)
            ss = jnp.dot((t * t).astype(BF16), bd, preferred_element_type=F32)
            t = t * lax.rsqrt(ss * (1.0 / HEAD_DIM) + NORM_EPS) * hg_ref[:, hg_off:hg_off + MXU_DIM]
            store_roped(t, lo, rp, PART_ROT // 2)
            hg_off += MXU_DIM

    store_roped(proj(OFF_CQ, OFF_CK), OFF_CQ, rr, HEAD_DIM // 2)
    store_roped(proj(OFF_CK, OFF_CV), OFF_CK, rr, HEAD_DIM // 2, scale=HEAD_DIM ** -0.5)

    for lo, hi in ((OFF_AV, OFF_BQ), (OFF_BV, OFF_CQ), (OFF_CV, N_MAIN)):
        main_ref[:, lo:hi] = proj(lo, hi).astype(BF16)

    for j in range(N_BRANCHES):
        lo = N_MAIN + j * D_MODEL
        gate_ref[:, j * D_MODEL:(j + 1) * D_MODEL] = jax.nn.sigmoid(proj(lo, lo + D_MODEL)).astype(BF16)


def _in_proj(x2, g, w_bf, bd, hg, rope_p, rope_r, seq):
    rows = x2.shape[0]
    tm = IN_TM
    per_seq = seq // tm
    return pl.pallas_call(
        _in_proj_kernel,
        grid=(rows // tm,),
        in_specs=[
            pl.BlockSpec((tm, D_MODEL), lambda i: (i, 0)),
            _resident((1, D_MODEL)),
            _resident((D_MODEL, N_IN)),
            _resident((MXU_DIM, MXU_DIM)),
            _resident((1, hg.shape[1])),
            pl.BlockSpec((3, tm, LANES), lambda i: (0, i % per_seq, 0)),
            pl.BlockSpec((3, tm, LANES), lambda i: (0, i % per_seq, 0)),
        ],
        out_specs=[
            pl.BlockSpec((tm, N_MAIN), lambda i: (i, 0)),
            pl.BlockSpec((tm, N_GATE), lambda i: (i, 0)),
        ],
        out_shape=[
            jax.ShapeDtypeStruct((rows, N_MAIN), BF16),
            jax.ShapeDtypeStruct((rows, N_GATE), BF16),
        ],
        compiler_params=_params("parallel"),
        name="in_proj",
    )(x2, g, w_bf, bd, hg, rope_p, rope_r)


def _dilated_kernel(q_ref, k_ref, v_ref, o_ref, lse_ref, *, length):
    tq = min(A_TQ, length)
    kw = min(tq + 2 * A_HALF, length)
    lane = lax.broadcasted_iota(jnp.int32, (1, LANES), 1)
    first = lane < HEAD_DIM
    row = lax.broadcasted_iota(jnp.int32, (tq, kw), 0)
    col = lax.broadcasted_iota(jnp.int32, (tq, kw), 1)
    scale = jnp.asarray(HEAD_DIM ** -0.5, BF16)

    def block(i, carry):
        q0 = pl.multiple_of(i * tq, tq)
        k0 = pl.multiple_of(jnp.clip(q0 - A_HALF, 0, length - kw), A_HALF)
        q = q_ref[0, pl.ds(q0, tq), :] * scale
        k = k_ref[0, pl.ds(k0, kw), :]
        v = v_ref[0, pl.ds(k0, kw), :]
        valid = jnp.abs((row + q0) - (col + k0)) <= A_HALF
        outs, lses = [], []
        for head_lanes in (first, jnp.logical_not(first)):
            qh = jnp.where(head_lanes, q, jnp.zeros_like(q))
            s = lax.dot_general(qh, k, (((1,), (1,)), ((), ())), preferred_element_type=F32)
            s = jnp.where(valid, s, NEG_INF)
            m = jnp.max(s, axis=-1, keepdims=True)
            e = jnp.exp(s - m)
            den = jnp.sum(e, axis=-1, keepdims=True)
            o = jnp.dot(e.astype(BF16), v, preferred_element_type=F32) / den
            outs.append(o)
            lses.append(jnp.broadcast_to(m + jnp.log(den), (tq, LANES)))
        o_ref[0, pl.ds(q0, tq), :] = jnp.where(first, outs[0], outs[1]).astype(o_ref.dtype)
        lse_ref[0, pl.ds(q0, tq), :] = jnp.where(first, lses[0], lses[1])
        return carry

    lax.fori_loop(0, length // tq, block, 0)


def _dilated_group(main3, group, dilation, batch, seq):
    length = seq // dilation
    blocks_per_pos = N_MAIN // LANES
    view = main3.reshape(batch, length, dilation * N_MAIN)

    def in_spec(col_block):
        return pl.BlockSpec((1, length, LANES), lambda b, r: (b, 0, r * blocks_per_pos + col_block))

    out_spec = pl.BlockSpec((1, length, LANES), lambda b, r: (b, 0, r))
    o, lse = pl.pallas_call(
        functools.partial(_dilated_kernel, length=length),
        grid=(batch, dilation),
        in_specs=[in_spec(OFF_AQ // LANES + group), in_spec(OFF_AK // LANES + group), in_spec(OFF_AV // LANES + group)],
        out_specs=[out_spec, out_spec],
        out_shape=[
            jax.ShapeDtypeStruct((batch, length, dilation * LANES), BF16),
            jax.ShapeDtypeStruct((batch, length, dilation * LANES), F32),
        ],
        compiler_params=_params("parallel", "parallel"),
        name=f"dilated_g{group}",
    )(view, view, view)
    return o.reshape(batch * seq, LANES), lse.reshape(batch * seq, LANES)


def _diff_kernel(lam_ref, q_ref, k_ref, v_ref, g_ref, o_ref, *, lam_init):
    lam_p = lam_ref[...]
    lam = (jnp.exp(jnp.sum(lam_p[0:1] * lam_p[1:2], axis=-1, keepdims=True))
           - jnp.exp(jnp.sum(lam_p[2:3] * lam_p[3:4], axis=-1, keepdims=True)) + lam_init)
    lane = lax.broadcasted_iota(jnp.int32, (1, LANES), 1)
    first = lane < HEAD_DIM
    q = q_ref[0] * jnp.asarray(HEAD_DIM ** -0.5, BF16)
    k = k_ref[0]
    zero = jnp.zeros_like(q)

    def softmax_parts(qm):
        s = lax.dot_general(qm, k, (((1,), (1,)), ((), ())), preferred_element_type=F32)
        m = jnp.max(s, axis=-1, keepdims=True)
        e = jnp.exp(s - m)
        return e, jnp.sum(e, axis=-1, keepdims=True)

    e1, d1 = softmax_parts(jnp.where(first, q, zero))
    e2, d2 = softmax_parts(jnp.where(first, zero, q))
    w = e1 * (1.0 / d1) - e2 * (lam / d2)
    o = jnp.dot(w.astype(BF16), v_ref[0], preferred_element_type=F32)
    ms = jnp.mean(o * o, axis=-1, keepdims=True)
    o_ref[0] = (o * lax.rsqrt(ms + NORM_EPS) * g_ref[...] * (1.0 - lam_init)).astype(o_ref.dtype)


def _diff_attention(main3, lam_p, out_g, lam_init, batch, seq):
    tq = B_TQ
    qb, kb, vb = OFF_BQ // LANES, OFF_BK // LANES, OFF_BV // LANES
    return pl.pallas_call(
        functools.partial(_diff_kernel, lam_init=lam_init),
        grid=(batch, B_HEADS, seq // tq),
        in_specs=[
            pl.BlockSpec((4, HEAD_DIM), lambda b, h, i: (0, 0)),
            pl.BlockSpec((1, tq, LANES), lambda b, h, i: (b, i, qb + h)),
            pl.BlockSpec((1, seq, LANES), lambda b, h, i: (b, 0, kb + h)),
            pl.BlockSpec((1, seq, LANES), lambda b, h, i: (b, 0, vb + h)),
            pl.BlockSpec((1, LANES), lambda b, h, i: (0, 0)),
        ],
        out_specs=pl.BlockSpec((1, tq, LANES), lambda b, h, i: (b, i, h)),
        out_shape=jax.ShapeDtypeStruct((batch, seq, B_WIDTH), BF16),
        compiler_params=_params("parallel", "parallel", "arbitrary"),
        name="diff_attention",
    )(lam_p, main3, main3, main3, out_g)


def _retention_kernel(dec_ref, q_ref, k_ref, v_ref, cg_ref, g_ref, o_ref, *, seq):
    cc = RET_CHUNK
    n = seq // cc
    h = pl.program_id(1)
    lane = lax.broadcasted_iota(jnp.int32, (1, LANES), 1)
    mine = (lane // HEAD_DIM) == (h % 2)

    dec = dec_ref[0]
    lg_f = -jnp.exp(dec[0:1, 0:1])
    lg_b = -jnp.exp(dec[1:2, 0:1])

    pos = lax.broadcasted_iota(jnp.int32, (cc, LANES), 0).astype(F32)
    dq_f = jnp.exp(lg_f * (pos + 1.0))
    dk_f = jnp.exp(lg_f * (cc - 1.0 - pos))
    dq_b = jnp.exp(lg_b * (cc - pos))
    dk_b = jnp.exp(lg_b * pos)
    chunk_f = jnp.exp(lg_f * cc)
    chunk_b = jnp.exp(lg_b * cc)
    rel = (lax.broadcasted_iota(jnp.int32, (cc, cc), 0) - lax.broadcasted_iota(jnp.int32, (cc, cc), 1)).astype(F32)
    decay = (jnp.where(rel >= 0, jnp.exp(lg_f * jnp.maximum(rel, 0.0)), 0.0)
             + jnp.where(rel <= 0, jnp.exp(lg_b * jnp.maximum(-rel, 0.0)), 0.0))

    def rows(ref, i):
        return ref[0, i * cc:(i + 1) * cc, :]

    def masked(ref, i):
        t = rows(ref, i)
        return jnp.where(mine, t, jnp.zeros_like(t))

    def kv_sum(i, dk):
        kd = (masked(k_ref, i).astype(F32) * dk).astype(BF16)
        return lax.dot_general(kd, rows(v_ref, i), (((0,), (0,)), ((), ())), preferred_element_type=F32)

    state = jnp.zeros((LANES, LANES), F32)
    fwd_states = []
    for i in range(n):
        fwd_states.append(state)
        if i + 1 < n:
            state = chunk_f * state + kv_sum(i, dk_f)
    state = jnp.zeros((LANES, LANES), F32)
    bwd_states = [None] * n
    for i in reversed(range(n)):
        bwd_states[i] = state
        if i > 0:
            state = chunk_b * state + kv_sum(i, dk_b)

    g = g_ref[...]
    for i in range(n):
        q = masked(q_ref, i)
        scores = lax.dot_general(q, masked(k_ref, i), (((1,), (1,)), ((), ())), preferred_element_type=F32)
        inner = jnp.dot((scores * decay).astype(BF16), rows(v_ref, i), preferred_element_type=F32)
        qf = q.astype(F32)
        q_cross = jnp.concatenate([(qf * dq_f).astype(BF16), (qf * dq_b).astype(BF16)], axis=1)
        s_cross = jnp.concatenate([fwd_states[i], bwd_states[i]], axis=0).astype(BF16)
        out = inner + jnp.dot(q_cross, s_cross, preferred_element_type=F32)
        ms = jnp.mean(out * out, axis=-1, keepdims=True)
        y = out * lax.rsqrt(ms + NORM_EPS) * g
        o_ref[0, i * cc:(i + 1) * cc, :] = (jax.nn.silu(rows(cg_ref, i).astype(F32)) * y).astype(o_ref.dtype)


def _retention(main3, dec, out_g, batch, seq):
    qb, kb, vb, gb = OFF_CQ // LANES, OFF_CK // LANES, OFF_CV // LANES, OFF_CG // LANES
    return pl.pallas_call(
        functools.partial(_retention_kernel, seq=seq),
        grid=(batch, C_HEADS),
        in_specs=[
            pl.BlockSpec((1, 2, LANES), lambda b, h: (h, 0, 0)),
            pl.BlockSpec((1, seq, LANES), lambda b, h: (b, 0, qb + h // 2)),
            pl.BlockSpec((1, seq, LANES), lambda b, h: (b, 0, kb + h // 2)),
            pl.BlockSpec((1, seq, LANES), lambda b, h: (b, 0, vb + h)),
            pl.BlockSpec((1, seq, LANES), lambda b, h: (b, 0, gb + h)),
            pl.BlockSpec((1, LANES), lambda b, h: (0, 0)),
        ],
        out_specs=pl.BlockSpec((1, seq, LANES), lambda b, h: (b, 0, h)),
        out_shape=jax.ShapeDtypeStruct((batch, seq, C_WIDTH), BF16),
        compiler_params=_params("parallel", "parallel"),
        name="retention",
    )(dec, main3, main3, main3, main3, out_g)


def _merge_mlp_kernel(x_ref, a0_ref, a1_ref, a2_ref, l0_ref, l1_ref, l2_ref, yb_ref, yc_ref, gate_ref,
                      wa_ref, wb_ref, wc_ref, wo_ref, g2_ref, w1_ref, w2_ref, o_ref):
    l0, l1, l2 = l0_ref[...], l1_ref[...], l2_ref[...]
    m = jnp.maximum(jnp.maximum(l0, l1), l2)
    e0, e1, e2 = jnp.exp(l0 - m), jnp.exp(l1 - m), jnp.exp(l2 - m)
    inv = 1.0 / (e0 + e1 + e2)
    pa = jnp.zeros((x_ref.shape[0], D_MODEL), F32)
    for j, (a_ref, e) in enumerate(((a0_ref, e0), (a1_ref, e1), (a2_ref, e2))):
        ya = (a_ref[...].astype(F32) * (e * inv)).astype(BF16)
        pa = pa + jnp.dot(ya, wa_ref[j * LANES:(j + 1) * LANES, :], preferred_element_type=F32)
    merged = gate_ref[:, 0:D_MODEL].astype(F32) * pa
    merged = merged + gate_ref[:, D_MODEL:2 * D_MODEL].astype(F32) * jnp.dot(
        yb_ref[...], wb_ref[...], preferred_element_type=F32)
    merged = merged + gate_ref[:, 2 * D_MODEL:3 * D_MODEL].astype(F32) * jnp.dot(
        yc_ref[...], wc_ref[...], preferred_element_type=F32)
    x1 = x_ref[...] + jnp.dot(merged.astype(BF16), wo_ref[...], preferred_element_type=F32)

    ms = jnp.mean(x1 * x1, axis=-1, keepdims=True)
    xn = (x1 * lax.rsqrt(ms + NORM_EPS) * g2_ref[...]).astype(BF16)
    acc = x1
    for c in range(D_FF // D_MODEL):
        lo, hi = c * D_MODEL, (c + 1) * D_MODEL
        hdn = jnp.dot(xn, w1_ref[:, lo:hi], preferred_element_type=F32)
        hdn = jnp.square(jnp.maximum(hdn, 0.0)).astype(BF16)
        acc = acc + jnp.dot(hdn, w2_ref[lo:hi, :], preferred_element_type=F32)
    o_ref[...] = acc


def _merge_mlp(x2, ya, lse, yb, yc, gates, wa, wb, wc, wo, g2, w1, w2):
    rows = x2.shape[0]
    tm = OUT_TM

    def row_block(width):
        return pl.BlockSpec((tm, width), lambda i: (i, 0))

    return pl.pallas_call(
        _merge_mlp_kernel,
        grid=(rows // tm,),
        in_specs=[row_block(D_MODEL)] + [row_block(LANES)] * 6 + [
            row_block(B_WIDTH), row_block(C_WIDTH), row_block(N_GATE),
            _resident(wa.shape), _resident(wb.shape), _resident(wc.shape), _resident(wo.shape),
            _resident(g2.shape), _resident(w1.shape), _resident(w2.shape)],
        out_specs=row_block(D_MODEL),
        out_shape=jax.ShapeDtypeStruct((rows, D_MODEL), F32),
        compiler_params=_params("parallel"),
        name="merge_mlp",
    )(x2, *ya, *lse, yb, yc, gates, wa, wb, wc, wo, g2, w1, w2)


def kernel(x, norm1_g, w_in, a_q_norm_g, a_k_norm_g, b_q_norm_g, b_k_norm_g, b_lambda_q1, b_lambda_k1, b_lambda_q2, b_lambda_k2, b_out_norm_g, c_decay_f, c_decay_b, c_out_norm_g, w_br_a, w_br_b, w_br_c, w_o, norm2_g, w_mlp1, w_mlp2):
    batch, seq, _ = x.shape
    depth = w_in.shape[0]
    rope_p = _rope_tables(seq, PART_ROT, ROPE_THETA, HEAD_DIM)
    rope_r = _rope_tables(seq, HEAD_DIM, RET_THETA, HEAD_DIM)
    head_id = np.arange(MXU_DIM) // HEAD_DIM
    bd = jnp.asarray(head_id[:, None] == head_id[None, :], BF16)

    x2 = x.reshape(batch * seq, D_MODEL)
    for l in range(depth):
        a_heads = A_WIDTH // HEAD_DIM
        b_heads2 = B_WIDTH // HEAD_DIM
        head_gain = jnp.concatenate([
            jnp.tile(a_q_norm_g[l], a_heads), jnp.tile(a_k_norm_g[l], a_heads),
            jnp.tile(b_q_norm_g[l], b_heads2), jnp.tile(b_k_norm_g[l], b_heads2)])[None, :]
        main, gates = _in_proj(x2, norm1_g[l][None, :], w_in[l].astype(BF16), bd, head_gain, rope_p, rope_r, seq)
        main3 = main.reshape(batch, seq, N_MAIN)

        ya, lse = [], []
        for group, (_, dilation) in enumerate(A_GROUPS):
            o, s = _dilated_group(main3, group, dilation, batch, seq)
            ya.append(o)
            lse.append(s)

        lam_init = 0.8 - 0.6 * math.exp(-0.3 * l)
        lam_p = jnp.stack([b_lambda_q1[l], b_lambda_k1[l], b_lambda_q2[l], b_lambda_k2[l]], axis=0)
        yb = _diff_attention(main3, lam_p, b_out_norm_g[l][None, :], lam_init, batch, seq)

        dec = jnp.broadcast_to(jnp.stack([c_decay_f[l], c_decay_b[l]], axis=1)[:, :, None], (C_HEADS, 2, LANES))
        yc = _retention(main3, dec, c_out_norm_g[l][None, :], batch, seq)

        x2 = _merge_mlp(
            x2, ya, lse, yb.reshape(batch * seq, B_WIDTH), yc.reshape(batch * seq, C_WIDTH), gates,
            w_br_a[l].astype(BF16), w_br_b[l].astype(BF16), w_br_c[l].astype(BF16), w_o[l].astype(BF16),
            norm2_g[l][None, :], w_mlp1[l].astype(BF16), w_mlp2[l].astype(BF16))
    return x2.reshape(batch, seq, D_MODEL)
```

```python
import functools
import math

import jax
import jax.numpy as jnp
import numpy as np
from jax import lax
from jax.experimental import pallas as pl
from jax.experimental.pallas import tpu as pltpu

F32 = jnp.float32
BF16 = jnp.bfloat16

LANES = 128
MXU_DIM = 256
VMEM_LIMIT_BYTES = 56 * 1024 * 1024

D_MODEL = 1024
HEAD_DIM = 64
A_GROUPS = ((128, 1), (512, 4), (2048, 16))
A_WIDTH = 384
B_HEADS = 4
B_WIDTH = 512
C_HEADS = 4
C_QK_WIDTH = 256
C_WIDTH = 512
D_FF = 4096
N_BRANCHES = 3
ROPE_THETA = 500000.0
RET_THETA = 10000.0
PART_ROT = 16
NORM_EPS = 1e-6
NEG_INF = -1e30

OFF_AQ, OFF_AK, OFF_AV = 0, 384, 768
OFF_BQ, OFF_BK, OFF_BV = 1152, 1664, 2176
OFF_CQ, OFF_CK, OFF_CV, OFF_CG = 2688, 2944, 3200, 3712
OFF_GATE = 4224
N_A = OFF_BQ
N_MAIN = OFF_GATE - N_A
N_GATE = N_BRANCHES * D_MODEL
N_IN = OFF_GATE + N_GATE

IN_TM = 512
OUT_TM = 256
B_TQ = 256
RET_CHUNK = 256
A_TQ = 128
A_HALF = 64


def _params(*sem):
    return pltpu.CompilerParams(dimension_semantics=sem, vmem_limit_bytes=VMEM_LIMIT_BYTES)


def _resident(shape):
    nd = len(shape)
    return pl.BlockSpec(shape, lambda *_: (0,) * nd, pipeline_mode=pl.Buffered(1))


def _rope_tables(seq, rot, theta, head_dim):
    half = rot // 2
    inv = 1.0 / (theta ** (jnp.arange(0, rot, 2, dtype=F32) / rot))
    ang = jnp.arange(seq, dtype=F32)[:, None] * inv[None, :]
    cos, sin = jnp.cos(ang), jnp.sin(ang)
    pad = head_dim - rot
    ones = jnp.ones((seq, pad), F32)
    zeros = jnp.zeros((seq, pad), F32)
    zh = jnp.zeros((seq, half), F32)
    c = jnp.concatenate([cos, cos, ones], axis=1)
    s_up = jnp.concatenate([zh, sin, zeros], axis=1)
    s_dn = jnp.concatenate([-sin, zh, zeros], axis=1)
    reps = LANES // head_dim
    return jnp.stack([jnp.tile(t, (1, reps)) for t in (c, s_up, s_dn)], axis=0)


def _apply_rope(t, tab, half):
    return (t * tab[0]
            + pltpu.roll(t, half, 1) * tab[1]
            + pltpu.roll(t, LANES - half, 1) * tab[2])


def _in_proj_kernel(x_ref, g_ref, w_ref, bd_ref, hg_ref, rp_ref, rr_ref, a_ref, main_ref, gate_ref):
    x = x_ref[...]
    ms = jnp.mean(x * x, axis=-1, keepdims=True)
    xn = (x * lax.rsqrt(ms + NORM_EPS) * g_ref[...]).astype(BF16)
    bd = bd_ref[...]
    rp = rp_ref[...]
    rr = rr_ref[...]

    def proj(lo, hi):
        return jnp.dot(xn, w_ref[:, lo:hi], preferred_element_type=F32)

    def store(lo, t):
        hi = lo + t.shape[1]
        if hi <= N_A:
            a_ref[:, lo:hi] = t
        else:
            main_ref[:, lo - N_A:hi - N_A] = t.astype(BF16)

    def store_roped(t, lo, tab, half, scale=None):
        for c in range(t.shape[1] // LANES):
            o = _apply_rope(t[:, c * LANES:(c + 1) * LANES], tab, half)
            if scale is not None:
                o = o * scale
            store(lo + c * LANES, o)

    hg_off = 0
    for seg_lo, seg_hi in ((OFF_AQ, OFF_AV), (OFF_BQ, OFF_BV)):
        for lo in range(seg_lo, seg_hi, MXU_DIM):
            t = proj(lo, lo + MXU_DIM)
            ss = jnp.dot((t * t).astype(BF16), bd, preferred_element_type=F32)
            t = t * lax.rsqrt(ss * (1.0 / HEAD_DIM) + NORM_EPS) * hg_ref[:, hg_off:hg_off + MXU_DIM]
            store_roped(t, lo, rp, PART_ROT // 2)
            hg_off += MXU_DIM

    store_roped(proj(OFF_CQ, OFF_CK), OFF_CQ, rr, HEAD_DIM // 2)
    store_roped(proj(OFF_CK, OFF_CV), OFF_CK, rr, HEAD_DIM // 2, scale=HEAD_DIM ** -0.5)

    for lo, hi in ((OFF_AV, OFF_BQ), (OFF_BV, OFF_CQ), (OFF_CV, OFF_GATE)):
        store(lo, proj(lo, hi))

    for j in range(N_BRANCHES):
        lo = OFF_GATE + j * D_MODEL
        gate_ref[:, j * D_MODEL:(j + 1) * D_MODEL] = jax.nn.sigmoid(proj(lo, lo + D_MODEL)).astype(BF16)


def _in_proj(x2, g, w_bf, bd, hg, rope_p, rope_r, seq):
    rows = x2.shape[0]
    tm = IN_TM
    per_seq = seq // tm
    return pl.pallas_call(
        _in_proj_kernel,
        grid=(rows // tm,),
        in_specs=[
            pl.BlockSpec((tm, D_MODEL), lambda i: (i, 0)),
            _resident((1, D_MODEL)),
            _resident((D_MODEL, N_IN)),
            _resident((MXU_DIM, MXU_DIM)),
            _resident((1, hg.shape[1])),
            pl.BlockSpec((3, tm, LANES), lambda i: (0, i % per_seq, 0)),
            pl.BlockSpec((3, tm, LANES), lambda i: (0, i % per_seq, 0)),
        ],
        out_specs=[
            pl.BlockSpec((tm, N_A), lambda i: (i, 0)),
            pl.BlockSpec((tm, N_MAIN), lambda i: (i, 0)),
            pl.BlockSpec((tm, N_GATE), lambda i: (i, 0)),
        ],
        out_shape=[
            jax.ShapeDtypeStruct((rows, N_A), F32),
            jax.ShapeDtypeStruct((rows, N_MAIN), BF16),
            jax.ShapeDtypeStruct((rows, N_GATE), BF16),
        ],
        compiler_params=_params("parallel"),
        name="in_proj",
    )(x2, g, w_bf, bd, hg, rope_p, rope_r)


def _dilated_kernel(*refs, seq):
    n_groups = len(A_GROUPS)
    q_refs, k_refs, v_refs = (refs[i * n_groups:(i + 1) * n_groups] for i in range(3))
    o_ref, qs, ks, vs, o_res, l_res, o_pos, l_pos = refs[3 * n_groups:]
    lane = lax.broadcasted_iota(jnp.int32, (1, LANES), 1)
    first = lane < HEAD_DIM
    scale = HEAD_DIM ** -0.5

    def cols(g):
        return slice(g * LANES, (g + 1) * LANES)

    for g, (_, dil) in enumerate(A_GROUPS):
        length = seq // dil
        for r in range(dil):
            src = pl.ds(r, length, stride=dil) if dil > 1 else slice(None)
            dst = slice(r * length, (r + 1) * length)
            qs[g, dst, :] = (q_refs[g][0, src, :] * scale).astype(BF16)
            ks[g, dst, :] = k_refs[g][0, src, :].astype(BF16)
            vs[g, dst, :] = v_refs[g][0, src, :].astype(BF16)

    for g, (_, dil) in enumerate(A_GROUPS):
        length = seq // dil
        tq = min(A_TQ, length)
        kw = min(tq + 2 * A_HALF, length)
        per_res = length // tq
        row = lax.broadcasted_iota(jnp.int32, (tq, kw), 0)
        col = lax.broadcasted_iota(jnp.int32, (tq, kw), 1)

        def block(t, carry, g=g, length=length, tq=tq, kw=kw, per_res=per_res, row=row, col=col):
            q0 = pl.multiple_of(t * tq, tq)
            n0 = (t % per_res) * tq
            k_off = jnp.clip(n0 - A_HALF, 0, length - kw)
            k0 = pl.multiple_of((t // per_res) * length + k_off, A_HALF)
            q = qs[g, pl.ds(q0, tq), :]
            k = ks[g, pl.ds(k0, kw), :]
            v = vs[g, pl.ds(k0, kw), :]
            valid = jnp.abs((row + n0) - (col + k_off)) <= A_HALF
            outs, lses = [], []
            for head_lanes in (first, jnp.logical_not(first)):
                qh = jnp.where(head_lanes, q, jnp.zeros_like(q))
                s = lax.dot_general(qh, k, (((1,), (1,)), ((), ())), preferred_element_type=F32)
                s = jnp.where(valid, s, NEG_INF)
                m = jnp.max(s, axis=-1, keepdims=True)
                e = jnp.exp(s - m)
                den = jnp.sum(e, axis=-1, keepdims=True)
                outs.append(jnp.dot(e.astype(BF16), v, preferred_element_type=F32) / den)
                lses.append(jnp.broadcast_to(m + jnp.log(den), (tq, LANES)))
            o_res[g, pl.ds(q0, tq), :] = jnp.where(first, outs[0], outs[1])
            l_res[g, pl.ds(q0, tq), :] = jnp.where(first, lses[0], lses[1])
            return carry

        lax.fori_loop(0, seq // tq, block, 0, unroll=2)

    for g, (_, dil) in enumerate(A_GROUPS):
        if dil == 1:
            continue
        length = seq // dil
        for r in range(dil):
            src = slice(r * length, (r + 1) * length)
            o_pos[g - 1, pl.ds(r, length, stride=dil), :] = o_res[g, src, :]
            l_pos[g - 1, pl.ds(r, length, stride=dil), :] = l_res[g, src, :]

    chunk = 256
    for c in range(seq // chunk):
        rows = slice(c * chunk, (c + 1) * chunk)
        ls = [l_res[0, rows, :]] + [l_pos[g - 1, rows, :] for g in range(1, n_groups)]
        os_ = [o_res[0, rows, :]] + [o_pos[g - 1, rows, :] for g in range(1, n_groups)]
        m = functools.reduce(jnp.maximum, ls)
        es = [jnp.exp(l - m) for l in ls]
        inv = 1.0 / functools.reduce(jnp.add, es)
        for g in range(n_groups):
            o_ref[0, rows, cols(g)] = (os_[g] * (es[g] * inv)).astype(o_ref.dtype)


def _dilated_attention(a3, batch, seq):
    assert A_GROUPS[0][1] == 1
    n_groups = len(A_GROUPS)

    def in_spec(col_block):
        return pl.BlockSpec((1, seq, LANES), lambda b: (b, 0, col_block))

    return pl.pallas_call(
        functools.partial(_dilated_kernel, seq=seq),
        grid=(batch,),
        in_specs=[in_spec(c) for c in range(3 * n_groups)],
        out_specs=pl.BlockSpec((1, seq, A_WIDTH), lambda b: (b, 0, 0)),
        out_shape=jax.ShapeDtypeStruct((batch, seq, A_WIDTH), BF16),
        scratch_shapes=[pltpu.VMEM((n_groups, seq, LANES), BF16)] * 3
        + [pltpu.VMEM((n_groups, seq, LANES), F32)] * 2
        + [pltpu.VMEM((n_groups - 1, seq, LANES), F32)] * 2,
        compiler_params=_params("parallel"),
        name="dilated_attention",
    )(*([a3] * (3 * n_groups)))


def _diff_kernel(lam_ref, q_ref, k_ref, v_ref, g_ref, o_ref, v1_ref, *, lam_init):
    lam_p = lam_ref[...]
    lam = (jnp.exp(jnp.sum(lam_p[0:1] * lam_p[1:2], axis=-1, keepdims=True))
           - jnp.exp(jnp.sum(lam_p[2:3] * lam_p[3:4], axis=-1, keepdims=True)) + lam_init)
    lane = lax.broadcasted_iota(jnp.int32, (1, LANES), 1)
    first = lane < HEAD_DIM
    q = q_ref[0] * jnp.asarray(HEAD_DIM ** -0.5, BF16)
    k = k_ref[0]
    zero = jnp.zeros_like(q)

    @pl.when(pl.program_id(2) == 0)
    def _():
        v1_ref[:, :LANES] = v_ref[0]
        v1_ref[:, LANES:] = jnp.ones((v1_ref.shape[0], LANES), BF16)

    def softmax_pv(qm):
        s = lax.dot_general(qm, k, (((1,), (1,)), ((), ())), preferred_element_type=F32)
        m = jnp.max(s, axis=-1, keepdims=True)
        e = jnp.exp(s - m).astype(BF16)
        pv = jnp.dot(e, v1_ref[...], preferred_element_type=F32)
        return pv[:, :LANES] / pv[:, LANES:]

    o = softmax_pv(jnp.where(first, q, zero)) - lam * softmax_pv(jnp.where(first, zero, q))
    ms = jnp.mean(o * o, axis=-1, keepdims=True)
    o_ref[0] = (o * lax.rsqrt(ms + NORM_EPS) * g_ref[...] * (1.0 - lam_init)).astype(o_ref.dtype)


def _diff_attention(main3, lam_p, out_g, lam_init, batch, seq):
    tq = B_TQ
    qb, kb, vb = ((off - N_A) // LANES for off in (OFF_BQ, OFF_BK, OFF_BV))
    return pl.pallas_call(
        functools.partial(_diff_kernel, lam_init=lam_init),
        grid=(batch, B_HEADS, seq // tq),
        in_specs=[
            pl.BlockSpec((4, HEAD_DIM), lambda b, h, i: (0, 0)),
            pl.BlockSpec((1, tq, LANES), lambda b, h, i: (b, i, qb + h)),
            pl.BlockSpec((1, seq, LANES), lambda b, h, i: (b, 0, kb + h)),
            pl.BlockSpec((1, seq, LANES), lambda b, h, i: (b, 0, vb + h)),
            pl.BlockSpec((1, LANES), lambda b, h, i: (0, 0)),
        ],
        out_specs=pl.BlockSpec((1, tq, LANES), lambda b, h, i: (b, i, h)),
        out_shape=jax.ShapeDtypeStruct((batch, seq, B_WIDTH), BF16),
        scratch_shapes=[pltpu.VMEM((seq, 2 * LANES), BF16)],
        compiler_params=_params("parallel", "parallel", "arbitrary"),
        name="diff_attention",
    )(lam_p, main3, main3, main3, out_g)


def _retention_kernel(dec_ref, q_ref, k_ref, v_ref, cg_ref, g_ref, o_ref, *, seq):
    cc = RET_CHUNK
    n = seq // cc
    h = pl.program_id(1)
    lane = lax.broadcasted_iota(jnp.int32, (1, LANES), 1)
    mine = (lane // HEAD_DIM) == (h % 2)

    dec = dec_ref[0]
    lg_f = -jnp.exp(dec[0:1, 0:1])
    lg_b = -jnp.exp(dec[1:2, 0:1])

    pos = lax.broadcasted_iota(jnp.int32, (cc, LANES), 0).astype(F32)
    dq_f = jnp.exp(lg_f * (pos + 1.0))
    dk_f = jnp.exp(lg_f * (cc - 1.0 - pos))
    dq_b = jnp.exp(lg_b * (cc - pos))
    dk_b = jnp.exp(lg_b * pos)
    chunk_f = jnp.exp(lg_f * cc)
    chunk_b = jnp.exp(lg_b * cc)
    rel = (lax.broadcasted_iota(jnp.int32, (cc, cc), 0) - lax.broadcasted_iota(jnp.int32, (cc, cc), 1)).astype(F32)
    decay = (jnp.where(rel >= 0, jnp.exp(lg_f * jnp.maximum(rel, 0.0)), 0.0)
             + jnp.where(rel <= 0, jnp.exp(lg_b * jnp.maximum(-rel, 0.0)), 0.0))

    def rows(ref, i):
        return ref[0, i * cc:(i + 1) * cc, :]

    def masked(ref, i):
        t = rows(ref, i)
        return jnp.where(mine, t, jnp.zeros_like(t))

    def kv_sum(i, dk):
        kd = (masked(k_ref, i).astype(F32) * dk).astype(BF16)
        return lax.dot_general(kd, rows(v_ref, i), (((0,), (0,)), ((), ())), preferred_element_type=F32)

    state = jnp.zeros((LANES, LANES), F32)
    fwd_states = []
    for i in range(n):
        fwd_states.append(state)
        if i + 1 < n:
            state = chunk_f * state + kv_sum(i, dk_f)
    state = jnp.zeros((LANES, LANES), F32)
    bwd_states = [None] * n
    for i in reversed(range(n)):
        bwd_states[i] = state
        if i > 0:
            state = chunk_b * state + kv_sum(i, dk_b)

    g = g_ref[...]
    for i in range(n):
        q = masked(q_ref, i)
        scores = lax.dot_general(q, masked(k_ref, i), (((1,), (1,)), ((), ())), preferred_element_type=F32)
        inner = jnp.dot((scores * decay).astype(BF16), rows(v_ref, i), preferred_element_type=F32)
        qf = q.astype(F32)
        q_cross = jnp.concatenate([(qf * dq_f).astype(BF16), (qf * dq_b).astype(BF16)], axis=1)
        s_cross = jnp.concatenate([fwd_states[i], bwd_states[i]], axis=0).astype(BF16)
        out = inner + jnp.dot(q_cross, s_cross, preferred_element_type=F32)
        ms = jnp.mean(out * out, axis=-1, keepdims=True)
        y = out * lax.rsqrt(ms + NORM_EPS) * g
        o_ref[0, i * cc:(i + 1) * cc, :] = (jax.nn.silu(rows(cg_ref, i).astype(F32)) * y).astype(o_ref.dtype)


def _retention(main3, dec, out_g, batch, seq):
    qb, kb, vb, gb = ((off - N_A) // LANES for off in (OFF_CQ, OFF_CK, OFF_CV, OFF_CG))
    return pl.pallas_call(
        functools.partial(_retention_kernel, seq=seq),
        grid=(batch, C_HEADS),
        in_specs=[
            pl.BlockSpec((1, 2, LANES), lambda b, h: (h, 0, 0)),
            pl.BlockSpec((1, seq, LANES), lambda b, h: (b, 0, qb + h // 2)),
            pl.BlockSpec((1, seq, LANES), lambda b, h: (b, 0, kb + h // 2)),
            pl.BlockSpec((1, seq, LANES), lambda b, h: (b, 0, vb + h)),
            pl.BlockSpec((1, seq, LANES), lambda b, h: (b, 0, gb + h)),
            pl.BlockSpec((1, LANES), lambda b, h: (0, 0)),
        ],
        out_specs=pl.BlockSpec((1, seq, LANES), lambda b, h: (b, 0, h)),
        out_shape=jax.ShapeDtypeStruct((batch, seq, C_WIDTH), BF16),
        compiler_params=_params("parallel", "parallel"),
        name="retention",
    )(dec, main3, main3, main3, main3, out_g)


def _merge_mlp_kernel(x_ref, ya_ref, yb_ref, yc_ref, gate_ref,
                      wa_ref, wb_ref, wc_ref, wo_ref, g2_ref, w1_ref, w2_ref, o_ref):
    merged = gate_ref[:, 0:D_MODEL].astype(F32) * jnp.dot(
        ya_ref[...], wa_ref[...], preferred_element_type=F32)
    merged = merged + gate_ref[:, D_MODEL:2 * D_MODEL].astype(F32) * jnp.dot(
        yb_ref[...], wb_ref[...], preferred_element_type=F32)
    merged = merged + gate_ref[:, 2 * D_MODEL:3 * D_MODEL].astype(F32) * jnp.dot(
        yc_ref[...], wc_ref[...], preferred_element_type=F32)
    x1 = x_ref[...] + jnp.dot(merged.astype(BF16), wo_ref[...], preferred_element_type=F32)

    ms = jnp.mean(x1 * x1, axis=-1, keepdims=True)
    xn = (x1 * lax.rsqrt(ms + NORM_EPS) * g2_ref[...]).astype(BF16)
    acc = x1
    for c in range(D_FF // D_MODEL):
        lo, hi = c * D_MODEL, (c + 1) * D_MODEL
        hdn = jnp.dot(xn, w1_ref[:, lo:hi], preferred_element_type=F32)
        hdn = jnp.square(jnp.maximum(hdn, 0.0)).astype(BF16)
        acc = acc + jnp.dot(hdn, w2_ref[lo:hi, :], preferred_element_type=F32)
    o_ref[...] = acc


def _merge_mlp(x2, ya, yb, yc, gates, wa, wb, wc, wo, g2, w1, w2):
    rows = x2.shape[0]
    tm = OUT_TM

    def row_block(width):
        return pl.BlockSpec((tm, width), lambda i: (i, 0))

    return pl.pallas_call(
        _merge_mlp_kernel,
        grid=(rows // tm,),
        in_specs=[
            row_block(D_MODEL), row_block(A_WIDTH), row_block(B_WIDTH), row_block(C_WIDTH), row_block(N_GATE),
            _resident(wa.shape), _resident(wb.shape), _resident(wc.shape), _resident(wo.shape),
            _resident(g2.shape), _resident(w1.shape), _resident(w2.shape)],
        out_specs=row_block(D_MODEL),
        out_shape=jax.ShapeDtypeStruct((rows, D_MODEL), F32),
        compiler_params=_params("parallel"),
        name="merge_mlp",
    )(x2, ya, yb, yc, gates, wa, wb, wc, wo, g2, w1, w2)


def kernel(x, norm1_g, w_in, a_q_norm_g, a_k_norm_g, b_q_norm_g, b_k_norm_g, b_lambda_q1, b_lambda_k1, b_lambda_q2, b_lambda_k2, b_out_norm_g, c_decay_f, c_decay_b, c_out_norm_g, w_br_a, w_br_b, w_br_c, w_o, norm2_g, w_mlp1, w_mlp2):
    batch, seq, _ = x.shape
    depth = w_in.shape[0]
    rope_p = _rope_tables(seq, PART_ROT, ROPE_THETA, HEAD_DIM)
    rope_r = _rope_tables(seq, HEAD_DIM, RET_THETA, HEAD_DIM)
    head_id = np.arange(MXU_DIM) // HEAD_DIM
    bd = jnp.asarray(head_id[:, None] == head_id[None, :], BF16)

    x2 = x.reshape(batch * seq, D_MODEL)
    for l in range(depth):
        a_heads = A_WIDTH // HEAD_DIM
        b_heads2 = B_WIDTH // HEAD_DIM
        head_gain = jnp.concatenate([
            jnp.tile(a_q_norm_g[l], a_heads), jnp.tile(a_k_norm_g[l], a_heads),
            jnp.tile(b_q_norm_g[l], b_heads2), jnp.tile(b_k_norm_g[l], b_heads2)])[None, :]
        a_qkv, main, gates = _in_proj(
            x2, norm1_g[l][None, :], w_in[l].astype(BF16), bd, head_gain, rope_p, rope_r, seq)
        main3 = main.reshape(batch, seq, N_MAIN)

        ya = _dilated_attention(a_qkv.reshape(batch, seq, N_A), batch, seq)

        lam_init = 0.8 - 0.6 * math.exp(-0.3 * l)
        lam_p = jnp.stack([b_lambda_q1[l], b_lambda_k1[l], b_lambda_q2[l], b_lambda_k2[l]], axis=0)
        yb = _diff_attention(main3, lam_p, b_out_norm_g[l][None, :], lam_init, batch, seq)

        dec = jnp.broadcast_to(jnp.stack([c_decay_f[l], c_decay_b[l]], axis=1)[:, :, None], (C_HEADS, 2, LANES))
        yc = _retention(main3, dec, c_out_norm_g[l][None, :], batch, seq)

        x2 = _merge_mlp(
            x2, ya.reshape(batch * seq, A_WIDTH), yb.reshape(batch * seq, B_WIDTH),
            yc.reshape(batch * seq, C_WIDTH), gates,
            w_br_a[l].astype(BF16), w_br_b[l].astype(BF16), w_br_c[l].astype(BF16), w_o[l].astype(BF16),
            norm2_g[l][None, :], w_mlp1[l].astype(BF16), w_mlp2[l].astype(BF16))
    return x2.reshape(batch, seq, D_MODEL)
```

```python
import functools
import math

import jax
import jax.numpy as jnp
import numpy as np
from jax import lax
from jax.experimental import pallas as pl
from jax.experimental.pallas import tpu as pltpu

F32 = jnp.float32
BF16 = jnp.bfloat16

LANES = 128
MXU_DIM = 256
VMEM_LIMIT_BYTES = 56 * 1024 * 1024

D_MODEL = 1024
HEAD_DIM = 64
A_GROUPS = ((128, 1), (512, 4), (2048, 16))
A_WIDTH = 384
B_HEADS = 4
B_WIDTH = 512
C_HEADS = 4
C_QK_WIDTH = 256
C_WIDTH = 512
D_FF = 4096
N_BRANCHES = 3
ROPE_THETA = 500000.0
RET_THETA = 10000.0
PART_ROT = 16
NORM_EPS = 1e-6
NEG_INF = -1e30

OFF_AQ, OFF_AK, OFF_AV = 0, 384, 768
OFF_BQ, OFF_BK, OFF_BV = 1152, 1664, 2176
OFF_CQ, OFF_CK, OFF_CV, OFF_CG = 2688, 2944, 3200, 3712
OFF_GATE = 4224
N_A = OFF_BQ
N_MAIN = OFF_GATE - N_A
N_GATE = N_BRANCHES * D_MODEL
N_IN = OFF_GATE + N_GATE

IN_TM = 512
OUT_TM = 512
B_TQ = 256
RET_CHUNK = 256
A_TQ = 128
A_HALF = 64


def _params(*sem):
    return pltpu.CompilerParams(dimension_semantics=sem, vmem_limit_bytes=VMEM_LIMIT_BYTES)


def _resident(shape):
    nd = len(shape)
    return pl.BlockSpec(shape, lambda *_: (0,) * nd, pipeline_mode=pl.Buffered(1))


def _rope_tables(seq, rot, theta, head_dim):
    half = rot // 2
    inv = 1.0 / (theta ** (jnp.arange(0, rot, 2, dtype=F32) / rot))
    ang = jnp.arange(seq, dtype=F32)[:, None] * inv[None, :]
    cos, sin = jnp.cos(ang), jnp.sin(ang)
    pad = head_dim - rot
    ones = jnp.ones((seq, pad), F32)
    zeros = jnp.zeros((seq, pad), F32)
    zh = jnp.zeros((seq, half), F32)
    c = jnp.concatenate([cos, cos, ones], axis=1)
    s_up = jnp.concatenate([zh, sin, zeros], axis=1)
    s_dn = jnp.concatenate([-sin, zh, zeros], axis=1)
    reps = LANES // head_dim
    return jnp.stack([jnp.tile(t, (1, reps)) for t in (c, s_up, s_dn)], axis=0)


def _apply_rope(t, tab, half):
    return (t * tab[0]
            + pltpu.roll(t, half, 1) * tab[1]
            + pltpu.roll(t, LANES - half, 1) * tab[2])


def _in_proj_kernel(x_ref, g_ref, w_ref, bd_ref, hg_ref, rp_ref, rr_ref, a_ref, main_ref, gate_ref):
    x = x_ref[...]
    ms = jnp.mean(x * x, axis=-1, keepdims=True)
    xn = (x * lax.rsqrt(ms + NORM_EPS) * g_ref[...]).astype(BF16)
    bd = bd_ref[...]
    rp = rp_ref[...]
    rr = rr_ref[...]

    def proj(lo, hi):
        return jnp.dot(xn, w_ref[:, lo:hi], preferred_element_type=F32)

    def store(lo, t):
        hi = lo + t.shape[1]
        if hi <= N_A:
            a_ref[:, lo:hi] = t
        else:
            main_ref[:, lo - N_A:hi - N_A] = t.astype(BF16)

    def store_roped(t, lo, tab, half, scale=None):
        for c in range(t.shape[1] // LANES):
            o = _apply_rope(t[:, c * LANES:(c + 1) * LANES], tab, half)
            if scale is not None:
                o = o * scale
            store(lo + c * LANES, o)

    hg_off = 0
    for seg_lo, seg_hi in ((OFF_AQ, OFF_AV), (OFF_BQ, OFF_BV)):
        seg = proj(seg_lo, seg_hi)
        for lo in range(0, seg_hi - seg_lo, MXU_DIM):
            t = seg[:, lo:lo + MXU_DIM]
            ss = jnp.dot((t * t).astype(BF16), bd, preferred_element_type=F32)
            t = t * lax.rsqrt(ss * (1.0 / HEAD_DIM) + NORM_EPS) * hg_ref[:, hg_off:hg_off + MXU_DIM]
            store_roped(t, seg_lo + lo, rp, PART_ROT // 2)
            hg_off += MXU_DIM

    seg = proj(OFF_CQ, OFF_CV)
    store_roped(seg[:, :C_QK_WIDTH], OFF_CQ, rr, HEAD_DIM // 2)
    store_roped(seg[:, C_QK_WIDTH:], OFF_CK, rr, HEAD_DIM // 2, scale=HEAD_DIM ** -0.5)

    for lo, hi in ((OFF_AV, OFF_BQ), (OFF_BV, OFF_CQ), (OFF_CV, OFF_GATE)):
        store(lo, proj(lo, hi))

    for j in range(N_BRANCHES):
        lo = OFF_GATE + j * D_MODEL
        gate_ref[:, j * D_MODEL:(j + 1) * D_MODEL] = jax.nn.sigmoid(proj(lo, lo + D_MODEL)).astype(BF16)


def _in_proj(x2, g, w_bf, bd, hg, rope_p, rope_r, seq):
    rows = x2.shape[0]
    tm = IN_TM
    per_seq = seq // tm
    return pl.pallas_call(
        _in_proj_kernel,
        grid=(rows // tm,),
        in_specs=[
            pl.BlockSpec((tm, D_MODEL), lambda i: (i, 0)),
            _resident((1, D_MODEL)),
            _resident((D_MODEL, N_IN)),
            _resident((MXU_DIM, MXU_DIM)),
            _resident((1, hg.shape[1])),
            pl.BlockSpec((3, tm, LANES), lambda i: (0, i % per_seq, 0)),
            pl.BlockSpec((3, tm, LANES), lambda i: (0, i % per_seq, 0)),
        ],
        out_specs=[
            pl.BlockSpec((tm, N_A), lambda i: (i, 0)),
            pl.BlockSpec((tm, N_MAIN), lambda i: (i, 0)),
            pl.BlockSpec((tm, N_GATE), lambda i: (i, 0)),
        ],
        out_shape=[
            jax.ShapeDtypeStruct((rows, N_A), F32),
            jax.ShapeDtypeStruct((rows, N_MAIN), BF16),
            jax.ShapeDtypeStruct((rows, N_GATE), BF16),
        ],
        compiler_params=_params("parallel"),
        name="in_proj",
    )(x2, g, w_bf, bd, hg, rope_p, rope_r)


def _dilated_kernel(*refs, seq):
    n_groups = len(A_GROUPS)
    q_refs, k_refs, v_refs = (refs[i * n_groups:(i + 1) * n_groups] for i in range(3))
    o_ref, qs, ks, vs, o_res, l_res, o_pos, l_pos = refs[3 * n_groups:]
    lane = lax.broadcasted_iota(jnp.int32, (1, LANES), 1)
    first = lane < HEAD_DIM
    scale = HEAD_DIM ** -0.5

    def cols(g):
        return slice(g * LANES, (g + 1) * LANES)

    for g, (_, dil) in enumerate(A_GROUPS):
        length = seq // dil
        for r in range(dil):
            src = pl.ds(r, length, stride=dil) if dil > 1 else slice(None)
            dst = slice(r * length, (r + 1) * length)
            qs[g, dst, :] = (q_refs[g][0, src, :] * scale).astype(BF16)
            ks[g, dst, :] = k_refs[g][0, src, :].astype(BF16)
            vs[g, dst, :LANES] = v_refs[g][0, src, :].astype(BF16)
        vs[g, :, LANES:] = jnp.ones((seq, LANES), BF16)

    for g, (_, dil) in enumerate(A_GROUPS):
        length = seq // dil
        tq = min(A_TQ, length)
        kw = min(tq + 2 * A_HALF, length)
        per_res = length // tq
        rel = (lax.broadcasted_iota(jnp.int32, (tq, kw), 0) - lax.broadcasted_iota(jnp.int32, (tq, kw), 1))
        masks = {}
        for t in range(seq // tq):
            n0 = (t % per_res) * tq
            k_off = min(max(n0 - A_HALF, 0), length - kw)
            k0 = (t // per_res) * length + k_off
            if n0 - k_off not in masks:
                masks[n0 - k_off] = jnp.abs(rel + (n0 - k_off)) <= A_HALF
            valid = masks[n0 - k_off]
            q = qs[g, t * tq:(t + 1) * tq, :]
            k = ks[g, k0:k0 + kw, :]
            v1 = vs[g, k0:k0 + kw, :]
            outs, lses = [], []
            for head_lanes in (first, jnp.logical_not(first)):
                qh = jnp.where(head_lanes, q, jnp.zeros_like(q))
                s = lax.dot_general(qh, k, (((1,), (1,)), ((), ())), preferred_element_type=F32)
                s = jnp.where(valid, s, NEG_INF)
                m = jnp.max(s, axis=-1, keepdims=True)
                e = jnp.exp(s - m).astype(BF16)
                pv = jnp.dot(e, v1, preferred_element_type=F32)
                den = pv[:, LANES:]
                outs.append(pv[:, :LANES] / den)
                lses.append(m + jnp.log(den))
            o_res[g, t * tq:(t + 1) * tq, :] = jnp.where(first, outs[0], outs[1])
            l_res[g, t * tq:(t + 1) * tq, :] = jnp.where(first, lses[0], lses[1])

    for g, (_, dil) in enumerate(A_GROUPS):
        if dil == 1:
            continue
        length = seq // dil
        for r in range(dil):
            src = slice(r * length, (r + 1) * length)
            o_pos[g - 1, pl.ds(r, length, stride=dil), :] = o_res[g, src, :]
            l_pos[g - 1, pl.ds(r, length, stride=dil), :] = l_res[g, src, :]

    chunk = 256
    for c in range(seq // chunk):
        rows = slice(c * chunk, (c + 1) * chunk)
        ls = [l_res[0, rows, :]] + [l_pos[g - 1, rows, :] for g in range(1, n_groups)]
        os_ = [o_res[0, rows, :]] + [o_pos[g - 1, rows, :] for g in range(1, n_groups)]
        m = functools.reduce(jnp.maximum, ls)
        es = [jnp.exp(l - m) for l in ls]
        inv = 1.0 / functools.reduce(jnp.add, es)
        for g in range(n_groups):
            o_ref[0, rows, cols(g)] = (os_[g] * (es[g] * inv)).astype(o_ref.dtype)


def _dilated_attention(a3, batch, seq):
    assert A_GROUPS[0][1] == 1
    n_groups = len(A_GROUPS)

    def in_spec(col_block):
        return pl.BlockSpec((1, seq, LANES), lambda b: (b, 0, col_block))

    return pl.pallas_call(
        functools.partial(_dilated_kernel, seq=seq),
        grid=(batch,),
        in_specs=[in_spec(c) for c in range(3 * n_groups)],
        out_specs=pl.BlockSpec((1, seq, A_WIDTH), lambda b: (b, 0, 0)),
        out_shape=jax.ShapeDtypeStruct((batch, seq, A_WIDTH), BF16),
        scratch_shapes=[pltpu.VMEM((n_groups, seq, LANES), BF16)] * 2
        + [pltpu.VMEM((n_groups, seq, 2 * LANES), BF16)]
        + [pltpu.VMEM((n_groups, seq, LANES), F32)] * 2
        + [pltpu.VMEM((n_groups - 1, seq, LANES), F32)] * 2,
        compiler_params=_params("parallel"),
        name="dilated_attention",
    )(*([a3] * (3 * n_groups)))


def _diff_kernel(lam_ref, q_ref, k_ref, v_ref, g_ref, o_ref, v1_ref, *, lam_init, seq):
    lam_p = lam_ref[...]
    lam = (jnp.exp(jnp.sum(lam_p[0:1] * lam_p[1:2], axis=-1, keepdims=True))
           - jnp.exp(jnp.sum(lam_p[2:3] * lam_p[3:4], axis=-1, keepdims=True)) + lam_init)
    lane = lax.broadcasted_iota(jnp.int32, (1, LANES), 1)
    first = lane < HEAD_DIM
    k = k_ref[0]
    gain = g_ref[...] * (1.0 - lam_init)

    v1_ref[:, :LANES] = v_ref[0]
    v1_ref[:, LANES:] = jnp.ones((seq, LANES), BF16)

    def softmax_pv(qm):
        s = lax.dot_general(qm, k, (((1,), (1,)), ((), ())), preferred_element_type=F32)
        m = jnp.max(s, axis=-1, keepdims=True)
        e = jnp.exp(s - m).astype(BF16)
        pv = jnp.dot(e, v1_ref[...], preferred_element_type=F32)
        return pv[:, :LANES] / pv[:, LANES:]

    for i in range(seq // B_TQ):
        rows = slice(i * B_TQ, (i + 1) * B_TQ)
        q = q_ref[0, rows, :] * jnp.asarray(HEAD_DIM ** -0.5, BF16)
        zero = jnp.zeros_like(q)
        o = softmax_pv(jnp.where(first, q, zero)) - lam * softmax_pv(jnp.where(first, zero, q))
        ms = jnp.mean(o * o, axis=-1, keepdims=True)
        o_ref[0, rows, :] = (o * lax.rsqrt(ms + NORM_EPS) * gain).astype(o_ref.dtype)


def _diff_attention(main3, lam_p, out_g, lam_init, batch, seq):
    qb, kb, vb = ((off - N_A) // LANES for off in (OFF_BQ, OFF_BK, OFF_BV))

    def head_block(col_block):
        return pl.BlockSpec((1, seq, LANES), lambda b, h: (b, 0, col_block + h))

    return pl.pallas_call(
        functools.partial(_diff_kernel, lam_init=lam_init, seq=seq),
        grid=(batch, B_HEADS),
        in_specs=[
            pl.BlockSpec((4, HEAD_DIM), lambda b, h: (0, 0)),
            head_block(qb), head_block(kb), head_block(vb),
            pl.BlockSpec((1, LANES), lambda b, h: (0, 0)),
        ],
        out_specs=head_block(0),
        out_shape=jax.ShapeDtypeStruct((batch, seq, B_WIDTH), BF16),
        scratch_shapes=[pltpu.VMEM((seq, 2 * LANES), BF16)],
        compiler_params=_params("parallel", "parallel"),
        name="diff_attention",
    )(lam_p, main3, main3, main3, out_g)


def _retention_kernel(dec_ref, q_ref, k_ref, v_ref, cg_ref, g_ref, o_ref, *, seq):
    cc = RET_CHUNK
    n = seq // cc
    h = pl.program_id(1)
    lane = lax.broadcasted_iota(jnp.int32, (1, LANES), 1)
    mine = (lane // HEAD_DIM) == (h % 2)

    dec = dec_ref[0]
    lg_f = -jnp.exp(dec[0:1, 0:1])
    lg_b = -jnp.exp(dec[1:2, 0:1])

    pos = lax.broadcasted_iota(jnp.int32, (cc, LANES), 0).astype(F32)
    dq_f = jnp.exp(lg_f * (pos + 1.0))
    dk_f = jnp.exp(lg_f * (cc - 1.0 - pos))
    dq_b = jnp.exp(lg_b * (cc - pos))
    dk_b = jnp.exp(lg_b * pos)
    chunk_f = jnp.exp(lg_f * cc)
    chunk_b = jnp.exp(lg_b * cc)
    rel = (lax.broadcasted_iota(jnp.int32, (cc, cc), 0) - lax.broadcasted_iota(jnp.int32, (cc, cc), 1)).astype(F32)
    decay = (jnp.where(rel >= 0, jnp.exp(lg_f * jnp.maximum(rel, 0.0)), 0.0)
             + jnp.where(rel <= 0, jnp.exp(lg_b * jnp.maximum(-rel, 0.0)), 0.0))

    def rows(ref, i):
        return ref[0, i * cc:(i + 1) * cc, :]

    def masked(ref, i):
        t = rows(ref, i)
        return jnp.where(mine, t, jnp.zeros_like(t))

    def kv_sum(i, dk):
        kd = (masked(k_ref, i).astype(F32) * dk).astype(BF16)
        return lax.dot_general(kd, rows(v_ref, i), (((0,), (0,)), ((), ())), preferred_element_type=F32)

    state = jnp.zeros((LANES, LANES), F32)
    fwd_states = []
    for i in range(n):
        fwd_states.append(state)
        if i + 1 < n:
            state = chunk_f * state + kv_sum(i, dk_f)
    state = jnp.zeros((LANES, LANES), F32)
    bwd_states = [None] * n
    for i in reversed(range(n)):
        bwd_states[i] = state
        if i > 0:
            state = chunk_b * state + kv_sum(i, dk_b)

    g = g_ref[...]
    for i in range(n):
        q = masked(q_ref, i)
        scores = lax.dot_general(q, masked(k_ref, i), (((1,), (1,)), ((), ())), preferred_element_type=F32)
        inner = jnp.dot((scores * decay).astype(BF16), rows(v_ref, i), preferred_element_type=F32)
        qf = q.astype(F32)
        q_cross = jnp.concatenate([(qf * dq_f).astype(BF16), (qf * dq_b).astype(BF16)], axis=1)
        s_cross = jnp.concatenate([fwd_states[i], bwd_states[i]], axis=0).astype(BF16)
        out = inner + jnp.dot(q_cross, s_cross, preferred_element_type=F32)
        ms = jnp.mean(out * out, axis=-1, keepdims=True)
        y = out * lax.rsqrt(ms + NORM_EPS) * g
        o_ref[0, i * cc:(i + 1) * cc, :] = (jax.nn.silu(rows(cg_ref, i).astype(F32)) * y).astype(o_ref.dtype)


def _retention(main3, dec, out_g, batch, seq):
    qb, kb, vb, gb = ((off - N_A) // LANES for off in (OFF_CQ, OFF_CK, OFF_CV, OFF_CG))
    return pl.pallas_call(
        functools.partial(_retention_kernel, seq=seq),
        grid=(batch, C_HEADS),
        in_specs=[
            pl.BlockSpec((1, 2, LANES), lambda b, h: (h, 0, 0)),
            pl.BlockSpec((1, seq, LANES), lambda b, h: (b, 0, qb + h // 2)),
            pl.BlockSpec((1, seq, LANES), lambda b, h: (b, 0, kb + h // 2)),
            pl.BlockSpec((1, seq, LANES), lambda b, h: (b, 0, vb + h)),
            pl.BlockSpec((1, seq, LANES), lambda b, h: (b, 0, gb + h)),
            pl.BlockSpec((1, LANES), lambda b, h: (0, 0)),
        ],
        out_specs=pl.BlockSpec((1, seq, LANES), lambda b, h: (b, 0, h)),
        out_shape=jax.ShapeDtypeStruct((batch, seq, C_WIDTH), BF16),
        compiler_params=_params("parallel", "parallel"),
        name="retention",
    )(dec, main3, main3, main3, main3, out_g)


def _merge_mlp_kernel(x_ref, ya_ref, yb_ref, yc_ref, gate_ref,
                      wa_ref, wb_ref, wc_ref, wo_ref, g2_ref, w1_ref, w2_ref, o_ref):
    merged = gate_ref[:, 0:D_MODEL].astype(F32) * jnp.dot(
        ya_ref[...], wa_ref[...], preferred_element_type=F32)
    merged = merged + gate_ref[:, D_MODEL:2 * D_MODEL].astype(F32) * jnp.dot(
        yb_ref[...], wb_ref[...], preferred_element_type=F32)
    merged = merged + gate_ref[:, 2 * D_MODEL:3 * D_MODEL].astype(F32) * jnp.dot(
        yc_ref[...], wc_ref[...], preferred_element_type=F32)
    x1 = x_ref[...] + jnp.dot(merged.astype(BF16), wo_ref[...], preferred_element_type=F32)

    ms = jnp.mean(x1 * x1, axis=-1, keepdims=True)
    xn = (x1 * lax.rsqrt(ms + NORM_EPS) * g2_ref[...]).astype(BF16)
    acc = x1
    for c in range(D_FF // D_MODEL):
        lo, hi = c * D_MODEL, (c + 1) * D_MODEL
        hdn = jnp.dot(xn, w1_ref[:, lo:hi], preferred_element_type=F32)
        hdn = jnp.square(jnp.maximum(hdn, 0.0)).astype(BF16)
        acc = acc + jnp.dot(hdn, w2_ref[lo:hi, :], preferred_element_type=F32)
    o_ref[...] = acc


def _merge_mlp(x2, ya, yb, yc, gates, wa, wb, wc, wo, g2, w1, w2):
    rows = x2.shape[0]
    tm = OUT_TM

    def row_block(width):
        return pl.BlockSpec((tm, width), lambda i: (i, 0))

    return pl.pallas_call(
        _merge_mlp_kernel,
        grid=(rows // tm,),
        in_specs=[
            row_block(D_MODEL), row_block(A_WIDTH), row_block(B_WIDTH), row_block(C_WIDTH), row_block(N_GATE),
            _resident(wa.shape), _resident(wb.shape), _resident(wc.shape), _resident(wo.shape),
            _resident(g2.shape), _resident(w1.shape), _resident(w2.shape)],
        out_specs=row_block(D_MODEL),
        out_shape=jax.ShapeDtypeStruct((rows, D_MODEL), F32),
        compiler_params=_params("parallel"),
        name="merge_mlp",
    )(x2, ya, yb, yc, gates, wa, wb, wc, wo, g2, w1, w2)


def kernel(x, norm1_g, w_in, a_q_norm_g, a_k_norm_g, b_q_norm_g, b_k_norm_g, b_lambda_q1, b_lambda_k1, b_lambda_q2, b_lambda_k2, b_out_norm_g, c_decay_f, c_decay_b, c_out_norm_g, w_br_a, w_br_b, w_br_c, w_o, norm2_g, w_mlp1, w_mlp2):
    batch, seq, _ = x.shape
    depth = w_in.shape[0]
    rope_p = _rope_tables(seq, PART_ROT, ROPE_THETA, HEAD_DIM)
    rope_r = _rope_tables(seq, HEAD_DIM, RET_THETA, HEAD_DIM)
    head_id = np.arange(MXU_DIM) // HEAD_DIM
    bd = jnp.asarray(head_id[:, None] == head_id[None, :], BF16)

    x2 = x.reshape(batch * seq, D_MODEL)
    for l in range(depth):
        a_heads = A_WIDTH // HEAD_DIM
        b_heads2 = B_WIDTH // HEAD_DIM
        head_gain = jnp.concatenate([
            jnp.tile(a_q_norm_g[l], a_heads), jnp.tile(a_k_norm_g[l], a_heads),
            jnp.tile(b_q_norm_g[l], b_heads2), jnp.tile(b_k_norm_g[l], b_heads2)])[None, :]
        a_qkv, main, gates = _in_proj(
            x2, norm1_g[l][None, :], w_in[l].astype(BF16), bd, head_gain, rope_p, rope_r, seq)
        main3 = main.reshape(batch, seq, N_MAIN)

        ya = _dilated_attention(a_qkv.reshape(batch, seq, N_A), batch, seq)

        lam_init = 0.8 - 0.6 * math.exp(-0.3 * l)
        lam_p = jnp.stack([b_lambda_q1[l], b_lambda_k1[l], b_lambda_q2[l], b_lambda_k2[l]], axis=0)
        yb = _diff_attention(main3, lam_p, b_out_norm_g[l][None, :], lam_init, batch, seq)

        dec = jnp.broadcast_to(jnp.stack([c_decay_f[l], c_decay_b[l]], axis=1)[:, :, None], (C_HEADS, 2, LANES))
        yc = _retention(main3, dec, c_out_norm_g[l][None, :], batch, seq)

        x2 = _merge_mlp(
            x2, ya.reshape(batch * seq, A_WIDTH), yb.reshape(batch * seq, B_WIDTH),
            yc.reshape(batch * seq, C_WIDTH), gates,
            w_br_a[l].astype(BF16), w_br_b[l].astype(BF16), w_br_c[l].astype(BF16), w_o[l].astype(BF16),
            norm2_g[l][None, :], w_mlp1[l].astype(BF16), w_mlp2[l].astype(BF16))
    return x2.reshape(batch, seq, D_MODEL)
```

```python
import functools
import math

import jax
import jax.numpy as jnp
import numpy as np
from jax import lax
from jax.experimental import pallas as pl
from jax.experimental.pallas import tpu as pltpu

F32 = jnp.float32
BF16 = jnp.bfloat16

LANES = 128
MXU_DIM = 256
VMEM_LIMIT_BYTES = 56 * 1024 * 1024

D_MODEL = 1024
HEAD_DIM = 64
A_GROUPS = ((128, 1), (512, 4), (2048, 16))
A_WIDTH = 384
B_HEADS = 4
B_WIDTH = 512
C_HEADS = 4
C_QK_WIDTH = 256
C_WIDTH = 512
D_FF = 4096
N_BRANCHES = 3
ROPE_THETA = 500000.0
RET_THETA = 10000.0
PART_ROT = 16
NORM_EPS = 1e-6
NEG_INF = -1e30

OFF_AQ, OFF_AK, OFF_AV = 0, 384, 768
OFF_BQ, OFF_BK, OFF_BV = 1152, 1664, 2176
OFF_CQ, OFF_CK, OFF_CV, OFF_CG = 2688, 2944, 3200, 3712
OFF_GATE = 4224
N_A = OFF_BQ
N_MAIN = OFF_GATE - N_A
N_GATE = N_BRANCHES * D_MODEL
N_IN = OFF_GATE + N_GATE

IN_TM = 512
OUT_TM = 512
B_TQ = 256
RET_CHUNK = 256
A_TQ = 128
A_HALF = 64


def _params(*sem):
    return pltpu.CompilerParams(dimension_semantics=sem, vmem_limit_bytes=VMEM_LIMIT_BYTES)


def _resident(shape):
    nd = len(shape)
    return pl.BlockSpec(shape, lambda *_: (0,) * nd, pipeline_mode=pl.Buffered(1))


def _layer_resident(stacked, layer):
    _, rows, cols = stacked.shape
    return pl.BlockSpec((None, rows, cols), lambda *_: (layer, 0, 0), pipeline_mode=pl.Buffered(1))


def _rope_tables(seq, rot, theta, head_dim):
    half = rot // 2
    inv = 1.0 / (theta ** (jnp.arange(0, rot, 2, dtype=F32) / rot))
    ang = jnp.arange(seq, dtype=F32)[:, None] * inv[None, :]
    cos, sin = jnp.cos(ang), jnp.sin(ang)
    pad = head_dim - rot
    ones = jnp.ones((seq, pad), F32)
    zeros = jnp.zeros((seq, pad), F32)
    zh = jnp.zeros((seq, half), F32)
    c = jnp.concatenate([cos, cos, ones], axis=1)
    s_up = jnp.concatenate([zh, sin, zeros], axis=1)
    s_dn = jnp.concatenate([-sin, zh, zeros], axis=1)
    reps = LANES // head_dim
    return jnp.stack([jnp.tile(t, (1, reps)) for t in (c, s_up, s_dn)], axis=0)


def _apply_rope(t, tab, half):
    return (t * tab[0]
            + pltpu.roll(t, half, 1) * tab[1]
            + pltpu.roll(t, LANES - half, 1) * tab[2])


def _in_proj_kernel(x_ref, g_ref, w_ref, bd_ref, hg_ref, rp_ref, rr_ref, a_ref, main_ref, gate_ref):
    x = x_ref[...]
    ms = jnp.mean(x * x, axis=-1, keepdims=True)
    xn = (x * lax.rsqrt(ms + NORM_EPS) * g_ref[...]).astype(BF16)
    bd = bd_ref[...]
    rp = rp_ref[...]
    rr = rr_ref[...]

    def proj(lo, hi):
        return jnp.dot(xn, w_ref[:, lo:hi].astype(BF16), preferred_element_type=F32)

    def store(lo, t):
        hi = lo + t.shape[1]
        if hi <= N_A:
            for c in range(t.shape[1] // LANES):
                a_ref[lo // LANES + c] = t[:, c * LANES:(c + 1) * LANES]
        else:
            main_ref[:, lo - N_A:hi - N_A] = t.astype(BF16)

    def store_roped(t, lo, tab, half, scale=None):
        for c in range(t.shape[1] // LANES):
            o = _apply_rope(t[:, c * LANES:(c + 1) * LANES], tab, half)
            if scale is not None:
                o = o * scale
            store(lo + c * LANES, o)

    hg_off = 0
    for seg_lo, seg_hi in ((OFF_AQ, OFF_AV), (OFF_BQ, OFF_BV)):
        seg = proj(seg_lo, seg_hi)
        for lo in range(0, seg_hi - seg_lo, MXU_DIM):
            t = seg[:, lo:lo + MXU_DIM]
            ss = jnp.dot((t * t).astype(BF16), bd, preferred_element_type=F32)
            t = t * lax.rsqrt(ss * (1.0 / HEAD_DIM) + NORM_EPS) * hg_ref[:, hg_off:hg_off + MXU_DIM]
            store_roped(t, seg_lo + lo, rp, PART_ROT // 2)
            hg_off += MXU_DIM

    seg = proj(OFF_CQ, OFF_CV)
    store_roped(seg[:, :C_QK_WIDTH], OFF_CQ, rr, HEAD_DIM // 2)
    store_roped(seg[:, C_QK_WIDTH:], OFF_CK, rr, HEAD_DIM // 2, scale=HEAD_DIM ** -0.5)

    for lo, hi in ((OFF_AV, OFF_BQ), (OFF_BV, OFF_CQ), (OFF_CV, OFF_GATE)):
        store(lo, proj(lo, hi))

    for j in range(N_BRANCHES):
        lo = OFF_GATE + j * D_MODEL
        gate_ref[:, j * D_MODEL:(j + 1) * D_MODEL] = jax.nn.sigmoid(proj(lo, lo + D_MODEL)).astype(BF16)


def _in_proj(x2, g, w_stack, layer, bd, hg, rope_p, rope_r, seq):
    rows = x2.shape[0]
    tm = IN_TM
    per_seq = seq // tm
    return pl.pallas_call(
        _in_proj_kernel,
        grid=(rows // tm,),
        in_specs=[
            pl.BlockSpec((tm, D_MODEL), lambda i: (i, 0)),
            _resident((1, D_MODEL)),
            _layer_resident(w_stack, layer),
            _resident((MXU_DIM, MXU_DIM)),
            _resident((1, hg.shape[1])),
            pl.BlockSpec((3, tm, LANES), lambda i: (0, i % per_seq, 0)),
            pl.BlockSpec((3, tm, LANES), lambda i: (0, i % per_seq, 0)),
        ],
        out_specs=[
            pl.BlockSpec((N_A // LANES, tm, LANES), lambda i: (0, i, 0)),
            pl.BlockSpec((tm, N_MAIN), lambda i: (i, 0)),
            pl.BlockSpec((tm, N_GATE), lambda i: (i, 0)),
        ],
        out_shape=[
            jax.ShapeDtypeStruct((N_A // LANES, rows, LANES), F32),
            jax.ShapeDtypeStruct((rows, N_MAIN), BF16),
            jax.ShapeDtypeStruct((rows, N_GATE), BF16),
        ],
        compiler_params=_params("parallel"),
        name="in_proj",
    )(x2, g, w_stack, bd, hg, rope_p, rope_r)


def _dilated_kernel(*refs, seq):
    n_groups = len(A_GROUPS)
    q_refs, k_refs, v_refs = (refs[i * n_groups:(i + 1) * n_groups] for i in range(3))
    o_ref, qs, ks, vs, o_res, l_res, o_pos, l_pos = refs[3 * n_groups:]
    lane = lax.broadcasted_iota(jnp.int32, (1, LANES), 1)
    first = lane < HEAD_DIM
    scale = HEAD_DIM ** -0.5

    def cols(g):
        return slice(g * LANES, (g + 1) * LANES)

    for g, (_, dil) in enumerate(A_GROUPS):
        length = seq // dil
        for r in range(dil):
            src = pl.ds(r, length, stride=dil) if dil > 1 else slice(None)
            dst = slice(r * length, (r + 1) * length)
            qs[g, dst, :] = (q_refs[g][0, src, :] * scale).astype(BF16)
            ks[g, dst, :] = k_refs[g][0, src, :].astype(BF16)
            vs[g, dst, :LANES] = v_refs[g][0, src, :].astype(BF16)
        vs[g, :, LANES:] = jnp.ones((seq, LANES), BF16)

    for g, (_, dil) in enumerate(A_GROUPS):
        length = seq // dil
        tq = min(A_TQ, length)
        kw = min(tq + 2 * A_HALF, length)
        per_res = length // tq
        rel = (lax.broadcasted_iota(jnp.int32, (tq, kw), 0) - lax.broadcasted_iota(jnp.int32, (tq, kw), 1))
        masks = {}
        for t in range(seq // tq):
            n0 = (t % per_res) * tq
            k_off = min(max(n0 - A_HALF, 0), length - kw)
            k0 = (t // per_res) * length + k_off
            if n0 - k_off not in masks:
                masks[n0 - k_off] = jnp.abs(rel + (n0 - k_off)) <= A_HALF
            valid = masks[n0 - k_off]
            q = qs[g, t * tq:(t + 1) * tq, :]
            k = ks[g, k0:k0 + kw, :]
            v1 = vs[g, k0:k0 + kw, :]
            outs, lses = [], []
            for head_lanes in (first, jnp.logical_not(first)):
                qh = jnp.where(head_lanes, q, jnp.zeros_like(q))
                s = lax.dot_general(qh, k, (((1,), (1,)), ((), ())), preferred_element_type=F32)
                s = jnp.where(valid, s, NEG_INF)
                m = jnp.max(s, axis=-1, keepdims=True)
                e = jnp.exp(s - m).astype(BF16)
                pv = jnp.dot(e, v1, preferred_element_type=F32)
                den = pv[:, LANES:]
                outs.append(pv[:, :LANES] / den)
                lses.append(m + jnp.log(den))
            o_res[g, t * tq:(t + 1) * tq, :] = jnp.where(first, outs[0], outs[1])
            l_res[g, t * tq:(t + 1) * tq, :] = jnp.where(first, lses[0], lses[1])

    for g, (_, dil) in enumerate(A_GROUPS):
        if dil == 1:
            continue
        length = seq // dil
        for r in range(dil):
            src = slice(r * length, (r + 1) * length)
            o_pos[g - 1, pl.ds(r, length, stride=dil), :] = o_res[g, src, :]
            l_pos[g - 1, pl.ds(r, length, stride=dil), :] = l_res[g, src, :]

    chunk = 256
    for c in range(seq // chunk):
        rows = slice(c * chunk, (c + 1) * chunk)
        ls = [l_res[0, rows, :]] + [l_pos[g - 1, rows, :] for g in range(1, n_groups)]
        os_ = [o_res[0, rows, :]] + [o_pos[g - 1, rows, :] for g in range(1, n_groups)]
        m = functools.reduce(jnp.maximum, ls)
        es = [jnp.exp(l - m) for l in ls]
        inv = 1.0 / functools.reduce(jnp.add, es)
        for g in range(n_groups):
            o_ref[0, rows, cols(g)] = (os_[g] * (es[g] * inv)).astype(o_ref.dtype)


def _dilated_attention(a4, batch, seq):
    assert A_GROUPS[0][1] == 1
    n_groups = len(A_GROUPS)

    def in_spec(slab):
        return pl.BlockSpec((None, 1, seq, LANES), lambda b: (slab, b, 0, 0))

    return pl.pallas_call(
        functools.partial(_dilated_kernel, seq=seq),
        grid=(batch,),
        in_specs=[in_spec(c) for c in range(3 * n_groups)],
        out_specs=pl.BlockSpec((1, seq, A_WIDTH), lambda b: (b, 0, 0)),
        out_shape=jax.ShapeDtypeStruct((batch, seq, A_WIDTH), BF16),
        scratch_shapes=[pltpu.VMEM((n_groups, seq, LANES), BF16)] * 2
        + [pltpu.VMEM((n_groups, seq, 2 * LANES), BF16)]
        + [pltpu.VMEM((n_groups, seq, LANES), F32)] * 2
        + [pltpu.VMEM((n_groups - 1, seq, LANES), F32)] * 2,
        compiler_params=_params("parallel"),
        name="dilated_attention",
    )(*([a4] * (3 * n_groups)))


def _diff_kernel(lam_ref, q_ref, k_ref, v_ref, g_ref, o_ref, v1_ref, *, lam_init, seq):
    lam_p = lam_ref[...]
    lam = (jnp.exp(jnp.sum(lam_p[0:1] * lam_p[1:2], axis=-1, keepdims=True))
           - jnp.exp(jnp.sum(lam_p[2:3] * lam_p[3:4], axis=-1, keepdims=True)) + lam_init)
    lane = lax.broadcasted_iota(jnp.int32, (1, LANES), 1)
    first = lane < HEAD_DIM
    k = k_ref[0]
    gain = g_ref[...] * (1.0 - lam_init)

    v1_ref[:, :LANES] = v_ref[0]
    v1_ref[:, LANES:] = jnp.ones((seq, LANES), BF16)

    def softmax_pv(qm):
        s = lax.dot_general(qm, k, (((1,), (1,)), ((), ())), preferred_element_type=F32)
        m = jnp.max(s, axis=-1, keepdims=True)
        e = jnp.exp(s - m).astype(BF16)
        pv = jnp.dot(e, v1_ref[...], preferred_element_type=F32)
        return pv[:, :LANES] / pv[:, LANES:]

    for i in range(seq // B_TQ):
        rows = slice(i * B_TQ, (i + 1) * B_TQ)
        q = q_ref[0, rows, :] * jnp.asarray(HEAD_DIM ** -0.5, BF16)
        zero = jnp.zeros_like(q)
        o = softmax_pv(jnp.where(first, q, zero)) - lam * softmax_pv(jnp.where(first, zero, q))
        ms = jnp.mean(o * o, axis=-1, keepdims=True)
        o_ref[0, rows, :] = (o * lax.rsqrt(ms + NORM_EPS) * gain).astype(o_ref.dtype)


def _diff_attention(main3, lam_p, out_g, lam_init, batch, seq):
    qb, kb, vb = ((off - N_A) // LANES for off in (OFF_BQ, OFF_BK, OFF_BV))

    def head_block(col_block):
        return pl.BlockSpec((1, seq, LANES), lambda b, h: (b, 0, col_block + h))

    return pl.pallas_call(
        functools.partial(_diff_kernel, lam_init=lam_init, seq=seq),
        grid=(batch, B_HEADS),
        in_specs=[
            pl.BlockSpec((4, HEAD_DIM), lambda b, h: (0, 0)),
            head_block(qb), head_block(kb), head_block(vb),
            pl.BlockSpec((1, LANES), lambda b, h: (0, 0)),
        ],
        out_specs=head_block(0),
        out_shape=jax.ShapeDtypeStruct((batch, seq, B_WIDTH), BF16),
        scratch_shapes=[pltpu.VMEM((seq, 2 * LANES), BF16)],
        compiler_params=_params("parallel", "parallel"),
        name="diff_attention",
    )(lam_p, main3, main3, main3, out_g)


def _retention_kernel(dec_ref, q_ref, k_ref, v_ref, cg_ref, g_ref, o_ref, *, seq):
    cc = RET_CHUNK
    n = seq // cc
    h = pl.program_id(1)
    lane = lax.broadcasted_iota(jnp.int32, (1, LANES), 1)
    mine = (lane // HEAD_DIM) == (h % 2)

    dec = dec_ref[0]
    lg_f = -jnp.exp(dec[0:1, 0:1])
    lg_b = -jnp.exp(dec[1:2, 0:1])

    pos = lax.broadcasted_iota(jnp.int32, (cc, LANES), 0).astype(F32)
    dq_f = jnp.exp(lg_f * (pos + 1.0))
    dk_f = jnp.exp(lg_f * (cc - 1.0 - pos))
    dq_b = jnp.exp(lg_b * (cc - pos))
    dk_b = jnp.exp(lg_b * pos)
    chunk_f = jnp.exp(lg_f * cc)
    chunk_b = jnp.exp(lg_b * cc)
    rel = (lax.broadcasted_iota(jnp.int32, (cc, cc), 0) - lax.broadcasted_iota(jnp.int32, (cc, cc), 1)).astype(F32)
    decay = (jnp.where(rel >= 0, jnp.exp(lg_f * jnp.maximum(rel, 0.0)), 0.0)
             + jnp.where(rel <= 0, jnp.exp(lg_b * jnp.maximum(-rel, 0.0)), 0.0))

    def rows(ref, i):
        return ref[0, i * cc:(i + 1) * cc, :]

    def masked(ref, i):
        t = rows(ref, i)
        return jnp.where(mine, t, jnp.zeros_like(t))

    def kv_sum(i, dk):
        kd = (masked(k_ref, i).astype(F32) * dk).astype(BF16)
        return lax.dot_general(kd, rows(v_ref, i), (((0,), (0,)), ((), ())), preferred_element_type=F32)

    state = jnp.zeros((LANES, LANES), F32)
    fwd_states = []
    for i in range(n):
        fwd_states.append(state)
        if i + 1 < n:
            state = chunk_f * state + kv_sum(i, dk_f)
    state = jnp.zeros((LANES, LANES), F32)
    bwd_states = [None] * n
    for i in reversed(range(n)):
        bwd_states[i] = state
        if i > 0:
            state = chunk_b * state + kv_sum(i, dk_b)

    g = g_ref[...]
    for i in range(n):
        q = masked(q_ref, i)
        scores = lax.dot_general(q, masked(k_ref, i), (((1,), (1,)), ((), ())), preferred_element_type=F32)
        inner = jnp.dot((scores * decay).astype(BF16), rows(v_ref, i), preferred_element_type=F32)
        qf = q.astype(F32)
        q_cross = jnp.concatenate([(qf * dq_f).astype(BF16), (qf * dq_b).astype(BF16)], axis=1)
        s_cross = jnp.concatenate([fwd_states[i], bwd_states[i]], axis=0).astype(BF16)
        out = inner + jnp.dot(q_cross, s_cross, preferred_element_type=F32)
        ms = jnp.mean(out * out, axis=-1, keepdims=True)
        y = out * lax.rsqrt(ms + NORM_EPS) * g
        o_ref[0, i * cc:(i + 1) * cc, :] = (jax.nn.silu(rows(cg_ref, i).astype(F32)) * y).astype(o_ref.dtype)


def _retention(main3, dec, out_g, batch, seq):
    qb, kb, vb, gb = ((off - N_A) // LANES for off in (OFF_CQ, OFF_CK, OFF_CV, OFF_CG))
    return pl.pallas_call(
        functools.partial(_retention_kernel, seq=seq),
        grid=(batch, C_HEADS),
        in_specs=[
            pl.BlockSpec((1, 2, LANES), lambda b, h: (h, 0, 0)),
            pl.BlockSpec((1, seq, LANES), lambda b, h: (b, 0, qb + h // 2)),
            pl.BlockSpec((1, seq, LANES), lambda b, h: (b, 0, kb + h // 2)),
            pl.BlockSpec((1, seq, LANES), lambda b, h: (b, 0, vb + h)),
            pl.BlockSpec((1, seq, LANES), lambda b, h: (b, 0, gb + h)),
            pl.BlockSpec((1, LANES), lambda b, h: (0, 0)),
        ],
        out_specs=pl.BlockSpec((1, seq, LANES), lambda b, h: (b, 0, h)),
        out_shape=jax.ShapeDtypeStruct((batch, seq, C_WIDTH), BF16),
        compiler_params=_params("parallel", "parallel"),
        name="retention",
    )(dec, main3, main3, main3, main3, out_g)


def _merge_mlp_kernel(x_ref, ya_ref, yb_ref, yc_ref, gate_ref,
                      wa_ref, wb_ref, wc_ref, wo_ref, g2_ref, w1_ref, w2_ref, o_ref):
    merged = gate_ref[:, 0:D_MODEL].astype(F32) * jnp.dot(
        ya_ref[...], wa_ref[...], preferred_element_type=F32)
    merged = merged + gate_ref[:, D_MODEL:2 * D_MODEL].astype(F32) * jnp.dot(
        yb_ref[...], wb_ref[...], preferred_element_type=F32)
    merged = merged + gate_ref[:, 2 * D_MODEL:3 * D_MODEL].astype(F32) * jnp.dot(
        yc_ref[...], wc_ref[...], preferred_element_type=F32)
    x1 = x_ref[...] + jnp.dot(merged.astype(BF16), wo_ref[...], preferred_element_type=F32)

    ms = jnp.mean(x1 * x1, axis=-1, keepdims=True)
    xn = (x1 * lax.rsqrt(ms + NORM_EPS) * g2_ref[...]).astype(BF16)
    acc = x1
    for c in range(D_FF // D_MODEL):
        lo, hi = c * D_MODEL, (c + 1) * D_MODEL
        hdn = jnp.dot(xn, w1_ref[:, lo:hi], preferred_element_type=F32)
        hdn = jnp.square(jnp.maximum(hdn, 0.0)).astype(BF16)
        acc = acc + jnp.dot(hdn, w2_ref[lo:hi, :], preferred_element_type=F32)
    o_ref[...] = acc


def _merge_mlp(x2, ya, yb, yc, gates, wa, wb, wc, wo, g2, w1, w2, layer):
    rows = x2.shape[0]
    tm = OUT_TM

    def row_block(width):
        return pl.BlockSpec((tm, width), lambda i: (i, 0))

    return pl.pallas_call(
        _merge_mlp_kernel,
        grid=(rows // tm,),
        in_specs=[
            row_block(D_MODEL), row_block(A_WIDTH), row_block(B_WIDTH), row_block(C_WIDTH), row_block(N_GATE),
            _layer_resident(wa, layer), _layer_resident(wb, layer), _layer_resident(wc, layer),
            _layer_resident(wo, layer), _resident(g2.shape), _layer_resident(w1, layer), _layer_resident(w2, layer)],
        out_specs=row_block(D_MODEL),
        out_shape=jax.ShapeDtypeStruct((rows, D_MODEL), F32),
        compiler_params=_params("parallel"),
        name="merge_mlp",
    )(x2, ya, yb, yc, gates, wa, wb, wc, wo, g2, w1, w2)


def kernel(x, norm1_g, w_in, a_q_norm_g, a_k_norm_g, b_q_norm_g, b_k_norm_g, b_lambda_q1, b_lambda_k1, b_lambda_q2, b_lambda_k2, b_out_norm_g, c_decay_f, c_decay_b, c_out_norm_g, w_br_a, w_br_b, w_br_c, w_o, norm2_g, w_mlp1, w_mlp2):
    batch, seq, _ = x.shape
    depth = w_in.shape[0]
    rope_p = _rope_tables(seq, PART_ROT, ROPE_THETA, HEAD_DIM)
    rope_r = _rope_tables(seq, HEAD_DIM, RET_THETA, HEAD_DIM)
    head_id = np.arange(MXU_DIM) // HEAD_DIM
    bd = jnp.asarray(head_id[:, None] == head_id[None, :], BF16)

    w_br_a, w_br_b, w_br_c, w_o, w_mlp1, w_mlp2 = (
        w.astype(BF16) for w in (w_br_a, w_br_b, w_br_c, w_o, w_mlp1, w_mlp2))

    x2 = x.reshape(batch * seq, D_MODEL)
    for l in range(depth):
        a_heads = A_WIDTH // HEAD_DIM
        b_heads2 = B_WIDTH // HEAD_DIM
        head_gain = jnp.concatenate([
            jnp.tile(a_q_norm_g[l], a_heads), jnp.tile(a_k_norm_g[l], a_heads),
            jnp.tile(b_q_norm_g[l], b_heads2), jnp.tile(b_k_norm_g[l], b_heads2)])[None, :]
        a_qkv, main, gates = _in_proj(
            x2, norm1_g[l][None, :], w_in, l, bd, head_gain, rope_p, rope_r, seq)
        main3 = main.reshape(batch, seq, N_MAIN)

        ya = _dilated_attention(a_qkv.reshape(N_A // LANES, batch, seq, LANES), batch, seq)

        lam_init = 0.8 - 0.6 * math.exp(-0.3 * l)
        lam_p = jnp.stack([b_lambda_q1[l], b_lambda_k1[l], b_lambda_q2[l], b_lambda_k2[l]], axis=0)
        yb = _diff_attention(main3, lam_p, b_out_norm_g[l][None, :], lam_init, batch, seq)

        dec = jnp.broadcast_to(jnp.stack([c_decay_f[l], c_decay_b[l]], axis=1)[:, :, None], (C_HEADS, 2, LANES))
        yc = _retention(main3, dec, c_out_norm_g[l][None, :], batch, seq)

        x2 = _merge_mlp(
            x2, ya.reshape(batch * seq, A_WIDTH), yb.reshape(batch * seq, B_WIDTH),
            yc.reshape(batch * seq, C_WIDTH), gates,
            w_br_a, w_br_b, w_br_c, w_o, norm2_g[l][None, :], w_mlp1, w_mlp2, l)
    return x2.reshape(batch, seq, D_MODEL)
```

```python
import functools
import math

import jax
import jax.numpy as jnp
import numpy as np
from jax import lax
from jax.experimental import pallas as pl
from jax.experimental.pallas import tpu as pltpu

F32 = jnp.float32
BF16 = jnp.bfloat16

LANES = 128
MXU_DIM = 256
VMEM_LIMIT_BYTES = 56 * 1024 * 1024

D_MODEL = 1024
HEAD_DIM = 64
A_GROUPS = ((128, 1), (512, 4), (2048, 16))
A_WIDTH = 384
B_HEADS = 4
B_WIDTH = 512
C_HEADS = 4
C_QK_WIDTH = 256
C_WIDTH = 512
D_FF = 4096
N_BRANCHES = 3
ROPE_THETA = 500000.0
RET_THETA = 10000.0
PART_ROT = 16
NORM_EPS = 1e-6
NEG_INF = -1e30

OFF_AQ, OFF_AK, OFF_AV = 0, 384, 768
OFF_BQ, OFF_BK, OFF_BV = 1152, 1664, 2176
OFF_CQ, OFF_CK, OFF_CV, OFF_CG = 2688, 2944, 3200, 3712
OFF_GATE = 4224
N_A = OFF_BQ
N_MAIN = OFF_GATE - N_A
N_GATE = N_BRANCHES * D_MODEL
N_IN = OFF_GATE + N_GATE

IN_TM = 512
OUT_TM = 512
B_TQ = 256
RET_CHUNK = 256
A_TQ = 128
A_HALF = 64


def _params(*sem):
    return pltpu.CompilerParams(dimension_semantics=sem, vmem_limit_bytes=VMEM_LIMIT_BYTES)


def _resident(shape):
    nd = len(shape)
    return pl.BlockSpec(shape, lambda *_: (0,) * nd, pipeline_mode=pl.Buffered(1))


def _layer_resident(stacked, layer):
    _, rows, cols = stacked.shape
    return pl.BlockSpec((None, rows, cols), lambda *_: (layer, 0, 0), pipeline_mode=pl.Buffered(1))


def _rope_tables(seq, rot, theta, head_dim):
    half = rot // 2
    inv = 1.0 / (theta ** (jnp.arange(0, rot, 2, dtype=F32) / rot))
    ang = jnp.arange(seq, dtype=F32)[:, None] * inv[None, :]
    cos, sin = jnp.cos(ang), jnp.sin(ang)
    pad = head_dim - rot
    ones = jnp.ones((seq, pad), F32)
    zeros = jnp.zeros((seq, pad), F32)
    zh = jnp.zeros((seq, half), F32)
    c = jnp.concatenate([cos, cos, ones], axis=1)
    s_up = jnp.concatenate([zh, sin, zeros], axis=1)
    s_dn = jnp.concatenate([-sin, zh, zeros], axis=1)
    reps = LANES // head_dim
    return jnp.stack([jnp.tile(t, (1, reps)) for t in (c, s_up, s_dn)], axis=0)


def _apply_rope(t, tab, half):
    return (t * tab[0]
            + pltpu.roll(t, half, 1) * tab[1]
            + pltpu.roll(t, LANES - half, 1) * tab[2])


def _in_proj_kernel(x_ref, g_ref, w_ref, bd_ref, hg_ref, rp_ref, rr_ref,
                    a0_ref, a1_ref, a2_ref, main_ref, gate_ref, stage_ref):
    a_refs = (a0_ref, a1_ref, a2_ref)
    n_groups = len(A_GROUPS)
    tm = x_ref.shape[0]
    x = x_ref[...]
    ms = jnp.mean(x * x, axis=-1, keepdims=True)
    xn = (x * lax.rsqrt(ms + NORM_EPS) * g_ref[...]).astype(BF16)
    bd = bd_ref[...]
    rp = rp_ref[...]
    rr = rr_ref[...]

    def proj(lo, hi):
        return jnp.dot(xn, w_ref[:, lo:hi].astype(BF16), preferred_element_type=F32)

    def store_a(slab, t):
        part, g = divmod(slab, n_groups)
        dil = A_GROUPS[g][1]
        if dil == 1:
            a_refs[g][part] = t.astype(BF16)
            return
        stage = stage_ref.at[part * (n_groups - 1) + g - 1]
        stage[...] = t
        for r in range(dil):
            a_refs[g][part, :, r * LANES:(r + 1) * LANES] = stage[pl.ds(r, tm // dil, stride=dil), :].astype(BF16)

    def store(lo, t):
        hi = lo + t.shape[1]
        if hi <= N_A:
            for c in range(t.shape[1] // LANES):
                store_a(lo // LANES + c, t[:, c * LANES:(c + 1) * LANES])
        else:
            main_ref[:, lo - N_A:hi - N_A] = t.astype(BF16)

    def store_roped(t, lo, tab, half, scale=None):
        for c in range(t.shape[1] // LANES):
            o = _apply_rope(t[:, c * LANES:(c + 1) * LANES], tab, half)
            if scale is not None:
                o = o * scale
            store(lo + c * LANES, o)

    hg_off = 0
    for seg_lo, seg_hi in ((OFF_AQ, OFF_AV), (OFF_BQ, OFF_BV)):
        seg = proj(seg_lo, seg_hi)
        for lo in range(0, seg_hi - seg_lo, MXU_DIM):
            t = seg[:, lo:lo + MXU_DIM]
            ss = jnp.dot((t * t).astype(BF16), bd, preferred_element_type=F32)
            t = t * lax.rsqrt(ss * (1.0 / HEAD_DIM) + NORM_EPS) * hg_ref[:, hg_off:hg_off + MXU_DIM]
            store_roped(t, seg_lo + lo, rp, PART_ROT // 2)
            hg_off += MXU_DIM

    seg = proj(OFF_CQ, OFF_CV)
    store_roped(seg[:, :C_QK_WIDTH], OFF_CQ, rr, HEAD_DIM // 2)
    store_roped(seg[:, C_QK_WIDTH:], OFF_CK, rr, HEAD_DIM // 2, scale=HEAD_DIM ** -0.5)

    for lo, hi in ((OFF_AV, OFF_BQ), (OFF_BV, OFF_CQ), (OFF_CV, OFF_GATE)):
        store(lo, proj(lo, hi))

    for j in range(N_BRANCHES):
        lo = OFF_GATE + j * D_MODEL
        gate_ref[:, j * D_MODEL:(j + 1) * D_MODEL] = jax.nn.sigmoid(proj(lo, lo + D_MODEL)).astype(BF16)


def _in_proj(x2, g, w_stack, layer, bd, hg, rope_p, rope_r, seq):
    rows = x2.shape[0]
    tm = IN_TM
    per_seq = seq // tm
    return pl.pallas_call(
        _in_proj_kernel,
        grid=(rows // tm,),
        in_specs=[
            pl.BlockSpec((tm, D_MODEL), lambda i: (i, 0)),
            _resident((1, D_MODEL)),
            _layer_resident(w_stack, layer),
            _resident((MXU_DIM, MXU_DIM)),
            _resident((1, hg.shape[1])),
            pl.BlockSpec((3, tm, LANES), lambda i: (0, i % per_seq, 0)),
            pl.BlockSpec((3, tm, LANES), lambda i: (0, i % per_seq, 0)),
        ],
        out_specs=[
            pl.BlockSpec((3, tm // dil, dil * LANES), lambda i: (0, i, 0)) for _, dil in A_GROUPS
        ] + [
            pl.BlockSpec((tm, N_MAIN), lambda i: (i, 0)),
            pl.BlockSpec((tm, N_GATE), lambda i: (i, 0)),
        ],
        out_shape=[
            jax.ShapeDtypeStruct((3, rows // dil, dil * LANES), BF16) for _, dil in A_GROUPS
        ] + [
            jax.ShapeDtypeStruct((rows, N_MAIN), BF16),
            jax.ShapeDtypeStruct((rows, N_GATE), BF16),
        ],
        scratch_shapes=[pltpu.VMEM((3 * (len(A_GROUPS) - 1), tm, LANES), F32)],
        compiler_params=_params("parallel"),
        name="in_proj",
    )(x2, g, w_stack, bd, hg, rope_p, rope_r)


def _dilated_kernel(a0_ref, a1_ref, a2_ref, o0_ref, l0_ref, o1_ref, l1_ref, o2_ref, l2_ref, *, seq):
    lane = lax.broadcasted_iota(jnp.int32, (1, LANES), 1)
    first = lane < HEAD_DIM
    for (_, dil), a_ref, o_ref, l_ref in zip(A_GROUPS, (a0_ref, a1_ref, a2_ref), (o0_ref, o1_ref, o2_ref),
                                             (l0_ref, l1_ref, l2_ref)):
        length = seq // dil
        tq = min(A_TQ, length)
        kw = min(tq + 2 * A_HALF, length)
        rel = (lax.broadcasted_iota(jnp.int32, (tq, kw), 0) - lax.broadcasted_iota(jnp.int32, (tq, kw), 1))
        ones = jnp.ones((kw, LANES), BF16)
        masks = {}
        for r in range(dil):
            lanes = slice(r * LANES, (r + 1) * LANES)
            for n0 in range(0, length, tq):
                k0 = min(max(n0 - A_HALF, 0), length - kw)
                if n0 - k0 not in masks:
                    masks[n0 - k0] = jnp.abs(rel + (n0 - k0)) <= A_HALF
                valid = masks[n0 - k0]
                q = a_ref[0, n0:n0 + tq, lanes]
                k = a_ref[1, k0:k0 + kw, lanes]
                v1 = jnp.concatenate([a_ref[2, k0:k0 + kw, lanes], ones], axis=1)
                outs, lses = [], []
                for head_lanes in (first, jnp.logical_not(first)):
                    qh = jnp.where(head_lanes, q, jnp.zeros_like(q))
                    s = lax.dot_general(qh, k, (((1,), (1,)), ((), ())), preferred_element_type=F32)
                    s = jnp.where(valid, s, NEG_INF)
                    m = jnp.max(s, axis=-1, keepdims=True)
                    e = jnp.exp(s - m).astype(BF16)
                    pv = jnp.dot(e, v1, preferred_element_type=F32)
                    den = pv[:, LANES:]
                    outs.append(pv[:, :LANES] / den)
                    lses.append(m + jnp.log(den))
                o_ref[n0:n0 + tq, lanes] = jnp.where(first, outs[0], outs[1])
                l_ref[n0:n0 + tq, lanes] = jnp.where(first, lses[0], lses[1])


def _dilated_attention(a_groups, batch, seq):
    in_specs, out_specs, out_shape = [], [], []
    for _, dil in A_GROUPS:
        length = seq // dil
        in_specs.append(pl.BlockSpec((3, length, dil * LANES), lambda b: (0, b, 0)))
        out_specs += [pl.BlockSpec((length, dil * LANES), lambda b: (b, 0))] * 2
        out_shape += [jax.ShapeDtypeStruct((batch * length, dil * LANES), F32)] * 2
    return pl.pallas_call(
        functools.partial(_dilated_kernel, seq=seq),
        grid=(batch,),
        in_specs=in_specs,
        out_specs=out_specs,
        out_shape=out_shape,
        compiler_params=_params("parallel"),
        name="dilated_attention",
    )(*a_groups)


def _diff_kernel(lam_ref, q_ref, k_ref, v_ref, g_ref, o_ref, v1_ref, *, lam_init, seq):
    lam_p = lam_ref[...]
    lam = (jnp.exp(jnp.sum(lam_p[0:1] * lam_p[1:2], axis=-1, keepdims=True))
           - jnp.exp(jnp.sum(lam_p[2:3] * lam_p[3:4], axis=-1, keepdims=True)) + lam_init)
    lane = lax.broadcasted_iota(jnp.int32, (1, LANES), 1)
    first = lane < HEAD_DIM
    k = k_ref[0]
    gain = g_ref[...] * (1.0 - lam_init)

    v1_ref[:, :LANES] = v_ref[0]
    v1_ref[:, LANES:] = jnp.ones((seq, LANES), BF16)

    def scores(i):
        q = q_ref[0, i * B_TQ:(i + 1) * B_TQ, :] * jnp.asarray(HEAD_DIM ** -0.5, BF16)
        zero = jnp.zeros_like(q)
        return [lax.dot_general(qm, k, (((1,), (1,)), ((), ())), preferred_element_type=F32)
                for qm in (jnp.where(first, q, zero), jnp.where(first, zero, q))]

    def softmax_pv(s):
        m = jnp.max(s, axis=-1, keepdims=True)
        e = jnp.exp(s - m).astype(BF16)
        pv = jnp.dot(e, v1_ref[...], preferred_element_type=F32)
        return pv[:, :LANES] / pv[:, LANES:]

    n = seq // B_TQ
    ahead = scores(0)
    for i in range(n):
        cur = ahead
        if i + 1 < n:
            ahead = scores(i + 1)
        rows = slice(i * B_TQ, (i + 1) * B_TQ)
        o = softmax_pv(cur[0]) - lam * softmax_pv(cur[1])
        ms = jnp.mean(o * o, axis=-1, keepdims=True)
        o_ref[0, rows, :] = (o * lax.rsqrt(ms + NORM_EPS) * gain).astype(o_ref.dtype)


def _diff_attention(main3, lam_p, out_g, lam_init, batch, seq):
    qb, kb, vb = ((off - N_A) // LANES for off in (OFF_BQ, OFF_BK, OFF_BV))

    def head_block(col_block):
        return pl.BlockSpec((1, seq, LANES), lambda b, h: (b, 0, col_block + h))

    return pl.pallas_call(
        functools.partial(_diff_kernel, lam_init=lam_init, seq=seq),
        grid=(batch, B_HEADS),
        in_specs=[
            pl.BlockSpec((4, HEAD_DIM), lambda b, h: (0, 0)),
            head_block(qb), head_block(kb), head_block(vb),
            pl.BlockSpec((1, LANES), lambda b, h: (0, 0)),
        ],
        out_specs=head_block(0),
        out_shape=jax.ShapeDtypeStruct((batch, seq, B_WIDTH), BF16),
        scratch_shapes=[pltpu.VMEM((seq, 2 * LANES), BF16)],
        compiler_params=_params("parallel", "parallel"),
        name="diff_attention",
    )(lam_p, main3, main3, main3, out_g)


def _retention_kernel(dec_ref, q_ref, k_ref, v_ref, cg_ref, g_ref, o_ref, *, seq):
    cc = RET_CHUNK
    n = seq // cc
    h = pl.program_id(1)
    lane = lax.broadcasted_iota(jnp.int32, (1, LANES), 1)
    mine = (lane // HEAD_DIM) == (h % 2)

    dec = dec_ref[0]
    lg_f = -jnp.exp(dec[0:1, 0:1])
    lg_b = -jnp.exp(dec[1:2, 0:1])

    pos = lax.broadcasted_iota(jnp.int32, (cc, LANES), 0).astype(F32)
    dq_f = jnp.exp(lg_f * (pos + 1.0))
    dk_f = jnp.exp(lg_f * (cc - 1.0 - pos))
    dq_b = jnp.exp(lg_b * (cc - pos))
    dk_b = jnp.exp(lg_b * pos)
    chunk_f = jnp.exp(lg_f * cc)
    chunk_b = jnp.exp(lg_b * cc)
    rel = (lax.broadcasted_iota(jnp.int32, (cc, cc), 0) - lax.broadcasted_iota(jnp.int32, (cc, cc), 1)).astype(F32)
    decay = (jnp.where(rel >= 0, jnp.exp(lg_f * jnp.maximum(rel, 0.0)), 0.0)
             + jnp.where(rel <= 0, jnp.exp(lg_b * jnp.maximum(-rel, 0.0)), 0.0))

    def rows(ref, i):
        return ref[0, i * cc:(i + 1) * cc, :]

    def masked(ref, i):
        t = rows(ref, i)
        return jnp.where(mine, t, jnp.zeros_like(t))

    def kv_sum(i, dk):
        kd = (rows(k_ref, i).astype(F32) * dk).astype(BF16)
        return lax.dot_general(kd, rows(v_ref, i), (((0,), (0,)), ((), ())), preferred_element_type=F32)

    state = jnp.zeros((LANES, LANES), F32)
    fwd_states = []
    for i in range(n):
        fwd_states.append(state)
        if i + 1 < n:
            state = chunk_f * state + kv_sum(i, dk_f)
    state = jnp.zeros((LANES, LANES), F32)
    bwd_states = [None] * n
    for i in reversed(range(n)):
        bwd_states[i] = state
        if i > 0:
            state = chunk_b * state + kv_sum(i, dk_b)

    g = g_ref[...]
    for i in range(n):
        q = masked(q_ref, i)
        scores = lax.dot_general(q, rows(k_ref, i), (((1,), (1,)), ((), ())), preferred_element_type=F32)
        inner = jnp.dot((scores * decay).astype(BF16), rows(v_ref, i), preferred_element_type=F32)
        qf = q.astype(F32)
        q_cross = jnp.concatenate([(qf * dq_f).astype(BF16), (qf * dq_b).astype(BF16)], axis=1)
        s_cross = jnp.concatenate([fwd_states[i], bwd_states[i]], axis=0).astype(BF16)
        out = inner + jnp.dot(q_cross, s_cross, preferred_element_type=F32)
        ms = jnp.mean(out * out, axis=-1, keepdims=True)
        y = out * lax.rsqrt(ms + NORM_EPS) * g
        o_ref[0, i * cc:(i + 1) * cc, :] = (jax.nn.silu(rows(cg_ref, i).astype(F32)) * y).astype(o_ref.dtype)


def _retention(main3, dec, out_g, batch, seq):
    qb, kb, vb, gb = ((off - N_A) // LANES for off in (OFF_CQ, OFF_CK, OFF_CV, OFF_CG))
    return pl.pallas_call(
        functools.partial(_retention_kernel, seq=seq),
        grid=(batch, C_HEADS),
        in_specs=[
            pl.BlockSpec((1, 2, LANES), lambda b, h: (h, 0, 0)),
            pl.BlockSpec((1, seq, LANES), lambda b, h: (b, 0, qb + h // 2)),
            pl.BlockSpec((1, seq, LANES), lambda b, h: (b, 0, kb + h // 2)),
            pl.BlockSpec((1, seq, LANES), lambda b, h: (b, 0, vb + h)),
            pl.BlockSpec((1, seq, LANES), lambda b, h: (b, 0, gb + h)),
            pl.BlockSpec((1, LANES), lambda b, h: (0, 0)),
        ],
        out_specs=pl.BlockSpec((1, seq, LANES), lambda b, h: (b, 0, h)),
        out_shape=jax.ShapeDtypeStruct((batch, seq, C_WIDTH), BF16),
        compiler_params=_params("parallel", "parallel"),
        name="retention",
    )(dec, main3, main3, main3, main3, out_g)


def _merge_mlp_kernel(x_ref, o0_ref, l0_ref, o1_ref, l1_ref, o2_ref, l2_ref, yb_ref, yc_ref, gate_ref,
                      wa_ref, wb_ref, wc_ref, wo_ref, g2_ref, w1_ref, w2_ref, o_ref, pos_ref):
    tm = x_ref.shape[0]
    os_, ls = [o0_ref[...]], [l0_ref[...]]
    slot = 0
    for (_, dil), o_in, l_in in zip(A_GROUPS[1:], (o1_ref, o2_ref), (l1_ref, l2_ref)):
        for src, dst in ((o_in, os_), (l_in, ls)):
            for r in range(dil):
                pos_ref[slot, pl.ds(r, tm // dil, stride=dil), :] = src[:, r * LANES:(r + 1) * LANES]
            dst.append(pos_ref[slot])
            slot += 1
    m = functools.reduce(jnp.maximum, ls)
    es = [jnp.exp(l - m) for l in ls]
    inv = 1.0 / functools.reduce(jnp.add, es)
    ya = jnp.concatenate([(o * (e * inv)).astype(BF16) for o, e in zip(os_, es)], axis=1)

    merged = gate_ref[:, 0:D_MODEL].astype(F32) * jnp.dot(ya, wa_ref[...], preferred_element_type=F32)
    merged = merged + gate_ref[:, D_MODEL:2 * D_MODEL].astype(F32) * jnp.dot(
        yb_ref[...], wb_ref[...], preferred_element_type=F32)
    merged = merged + gate_ref[:, 2 * D_MODEL:3 * D_MODEL].astype(F32) * jnp.dot(
        yc_ref[...], wc_ref[...], preferred_element_type=F32)
    x1 = x_ref[...] + jnp.dot(merged.astype(BF16), wo_ref[...], preferred_element_type=F32)

    ms = jnp.mean(x1 * x1, axis=-1, keepdims=True)
    xn = (x1 * lax.rsqrt(ms + NORM_EPS) * g2_ref[...]).astype(BF16)
    acc = x1
    for c in range(D_FF // D_MODEL):
        lo, hi = c * D_MODEL, (c + 1) * D_MODEL
        hdn = jnp.dot(xn, w1_ref[:, lo:hi], preferred_element_type=F32)
        hdn = jnp.square(jnp.maximum(hdn, 0.0)).astype(BF16)
        acc = acc + jnp.dot(hdn, w2_ref[lo:hi, :], preferred_element_type=F32)
    o_ref[...] = acc


def _merge_mlp(x2, ya_parts, yb, yc, gates, wa, wb, wc, wo, g2, w1, w2, layer):
    rows = x2.shape[0]
    tm = OUT_TM

    def row_block(width):
        return pl.BlockSpec((tm, width), lambda i: (i, 0))

    a_specs = []
    for _, dil in A_GROUPS:
        a_specs += [pl.BlockSpec((tm // dil, dil * LANES), lambda i: (i, 0))] * 2

    return pl.pallas_call(
        _merge_mlp_kernel,
        grid=(rows // tm,),
        in_specs=[row_block(D_MODEL)] + a_specs + [
            row_block(B_WIDTH), row_block(C_WIDTH), row_block(N_GATE),
            _layer_resident(wa, layer), _layer_resident(wb, layer), _layer_resident(wc, layer),
            _layer_resident(wo, layer), _resident(g2.shape), _layer_resident(w1, layer), _layer_resident(w2, layer)],
        out_specs=row_block(D_MODEL),
        out_shape=jax.ShapeDtypeStruct((rows, D_MODEL), F32),
        scratch_shapes=[pltpu.VMEM((2 * (len(A_GROUPS) - 1), tm, LANES), F32)],
        compiler_params=_params("parallel"),
        name="merge_mlp",
    )(x2, *ya_parts, yb, yc, gates, wa, wb, wc, wo, g2, w1, w2)


def kernel(x, norm1_g, w_in, a_q_norm_g, a_k_norm_g, b_q_norm_g, b_k_norm_g, b_lambda_q1, b_lambda_k1, b_lambda_q2, b_lambda_k2, b_out_norm_g, c_decay_f, c_decay_b, c_out_norm_g, w_br_a, w_br_b, w_br_c, w_o, norm2_g, w_mlp1, w_mlp2):
    batch, seq, _ = x.shape
    depth = w_in.shape[0]
    rope_p = _rope_tables(seq, PART_ROT, ROPE_THETA, HEAD_DIM)
    rope_r = _rope_tables(seq, HEAD_DIM, RET_THETA, HEAD_DIM)
    head_id = np.arange(MXU_DIM) // HEAD_DIM
    bd = jnp.asarray(head_id[:, None] == head_id[None, :], BF16)

    w_br_a, w_br_b, w_br_c, w_o, w_mlp1, w_mlp2 = (
        w.astype(BF16) for w in (w_br_a, w_br_b, w_br_c, w_o, w_mlp1, w_mlp2))

    x2 = x.reshape(batch * seq, D_MODEL)
    for l in range(depth):
        a_heads = A_WIDTH // HEAD_DIM
        b_heads2 = B_WIDTH // HEAD_DIM
        head_gain = jnp.concatenate([
            jnp.tile(a_q_norm_g[l] * HEAD_DIM ** -0.5, a_heads), jnp.tile(a_k_norm_g[l], a_heads),
            jnp.tile(b_q_norm_g[l], b_heads2), jnp.tile(b_k_norm_g[l], b_heads2)])[None, :]
        a0, a1, a2, main, gates = _in_proj(
            x2, norm1_g[l][None, :], w_in, l, bd, head_gain, rope_p, rope_r, seq)
        main3 = main.reshape(batch, seq, N_MAIN)

        ya_parts = _dilated_attention((a0, a1, a2), batch, seq)

        lam_init = 0.8 - 0.6 * math.exp(-0.3 * l)
        lam_p = jnp.stack([b_lambda_q1[l], b_lambda_k1[l], b_lambda_q2[l], b_lambda_k2[l]], axis=0)
        yb = _diff_attention(main3, lam_p, b_out_norm_g[l][None, :], lam_init, batch, seq)

        dec = jnp.broadcast_to(jnp.stack([c_decay_f[l], c_decay_b[l]], axis=1)[:, :, None], (C_HEADS, 2, LANES))
        yc = _retention(main3, dec, c_out_norm_g[l][None, :], batch, seq)

        x2 = _merge_mlp(
            x2, ya_parts, yb.reshape(batch * seq, B_WIDTH), yc.reshape(batch * seq, C_WIDTH), gates,
            w_br_a, w_br_b, w_br_c, w_o, norm2_g[l][None, :], w_mlp1, w_mlp2, l)
    return x2.reshape(batch, seq, D_MODEL)
```

```python
import functools
import math

import jax
import jax.numpy as jnp
import numpy as np
from jax import lax
from jax.experimental import pallas as pl
from jax.experimental.pallas import tpu as pltpu

F32 = jnp.float32
BF16 = jnp.bfloat16

LANES = 128
MXU_DIM = 256
VMEM_LIMIT_BYTES = 56 * 1024 * 1024

D_MODEL = 1024
HEAD_DIM = 64
A_GROUPS = ((128, 1), (512, 4), (2048, 16))
A_WIDTH = 384
B_HEADS = 4
B_WIDTH = 512
C_HEADS = 4
C_QK_WIDTH = 256
C_WIDTH = 512
D_FF = 4096
N_BRANCHES = 3
ROPE_THETA = 500000.0
RET_THETA = 10000.0
PART_ROT = 16
NORM_EPS = 1e-6
NEG_INF = -1e30

OFF_AQ, OFF_AK, OFF_AV = 0, 384, 768
OFF_BQ, OFF_BK, OFF_BV = 1152, 1664, 2176
OFF_CQ, OFF_CK, OFF_CV, OFF_CG = 2688, 2944, 3200, 3712
OFF_GATE = 4224
N_A = OFF_BQ
N_MAIN = OFF_GATE - N_A
N_GATE = N_BRANCHES * D_MODEL
N_IN = OFF_GATE + N_GATE

IN_TM = 512
OUT_TM = 512
B_TQ = 256
RET_CHUNK = 256
A_TQ = 128
A_HALF = 64


def _params(*sem):
    return pltpu.CompilerParams(dimension_semantics=sem, vmem_limit_bytes=VMEM_LIMIT_BYTES)


def _resident(shape):
    nd = len(shape)
    return pl.BlockSpec(shape, lambda *_: (0,) * nd, pipeline_mode=pl.Buffered(1))


def _layer_resident(stacked, layer):
    _, rows, cols = stacked.shape
    return pl.BlockSpec((None, rows, cols), lambda *_: (layer, 0, 0), pipeline_mode=pl.Buffered(1))


def _rope_tables(seq, rot, theta, head_dim):
    half = rot // 2
    inv = 1.0 / (theta ** (jnp.arange(0, rot, 2, dtype=F32) / rot))
    ang = jnp.arange(seq, dtype=F32)[:, None] * inv[None, :]
    cos, sin = jnp.cos(ang), jnp.sin(ang)
    pad = head_dim - rot
    ones = jnp.ones((seq, pad), F32)
    zeros = jnp.zeros((seq, pad), F32)
    zh = jnp.zeros((seq, half), F32)
    c = jnp.concatenate([cos, cos, ones], axis=1)
    s_up = jnp.concatenate([zh, sin, zeros], axis=1)
    s_dn = jnp.concatenate([-sin, zh, zeros], axis=1)
    reps = LANES // head_dim
    return jnp.stack([jnp.tile(t, (1, reps)) for t in (c, s_up, s_dn)], axis=0)


def _apply_rope(t, tab, half):
    return (t * tab[0]
            + pltpu.roll(t, half, 1) * tab[1]
            + pltpu.roll(t, LANES - half, 1) * tab[2])


def _in_proj_kernel(x_ref, g_ref, w_ref, bd_ref, hg_ref, rp_ref, rr_ref,
                    a0_ref, a1_ref, a2_ref, main_ref, gate_ref, stage_ref):
    a_refs = (a0_ref, a1_ref, a2_ref)
    n_groups = len(A_GROUPS)
    tm = x_ref.shape[0]
    x = x_ref[...]
    ms = jnp.mean(x * x, axis=-1, keepdims=True)
    xn = (x * lax.rsqrt(ms + NORM_EPS) * g_ref[...]).astype(BF16)
    bd = bd_ref[...]
    rp = rp_ref[...]
    rr = rr_ref[...]

    def proj(lo, hi):
        return jnp.dot(xn, w_ref[:, lo:hi].astype(BF16), preferred_element_type=F32)

    def store_a(slab, t):
        part, g = divmod(slab, n_groups)
        dil = A_GROUPS[g][1]
        if dil == 1:
            a_refs[g][part] = t.astype(BF16)
            return
        stage = stage_ref.at[part * (n_groups - 1) + g - 1]
        stage[...] = t
        for r in range(dil):
            a_refs[g][part, :, r * LANES:(r + 1) * LANES] = stage[pl.ds(r, tm // dil, stride=dil), :].astype(BF16)

    def store(lo, t):
        hi = lo + t.shape[1]
        if hi <= N_A:
            for c in range(t.shape[1] // LANES):
                store_a(lo // LANES + c, t[:, c * LANES:(c + 1) * LANES])
        else:
            main_ref[:, lo - N_A:hi - N_A] = t.astype(BF16)

    def store_roped(t, lo, tab, half, scale=None):
        for c in range(t.shape[1] // LANES):
            o = _apply_rope(t[:, c * LANES:(c + 1) * LANES], tab, half)
            if scale is not None:
                o = o * scale
            store(lo + c * LANES, o)

    hg_off = 0
    for seg_lo, seg_hi in ((OFF_AQ, OFF_AV), (OFF_BQ, OFF_BV)):
        seg = proj(seg_lo, seg_hi)
        for lo in range(0, seg_hi - seg_lo, MXU_DIM):
            t = seg[:, lo:lo + MXU_DIM]
            ss = jnp.dot((t * t).astype(BF16), bd, preferred_element_type=F32)
            t = t * lax.rsqrt(ss * (1.0 / HEAD_DIM) + NORM_EPS) * hg_ref[:, hg_off:hg_off + MXU_DIM]
            store_roped(t, seg_lo + lo, rp, PART_ROT // 2)
            hg_off += MXU_DIM

    seg = proj(OFF_CQ, OFF_CV)
    store_roped(seg[:, :C_QK_WIDTH], OFF_CQ, rr, HEAD_DIM // 2)
    store_roped(seg[:, C_QK_WIDTH:], OFF_CK, rr, HEAD_DIM // 2, scale=HEAD_DIM ** -0.5)

    for lo, hi in ((OFF_AV, OFF_BQ), (OFF_BV, OFF_CQ), (OFF_CV, OFF_CG)):
        store(lo, proj(lo, hi))
    store(OFF_CG, jax.nn.silu(proj(OFF_CG, OFF_GATE)))

    for j in range(N_BRANCHES):
        lo = OFF_GATE + j * D_MODEL
        gate_ref[:, j * D_MODEL:(j + 1) * D_MODEL] = jax.nn.sigmoid(proj(lo, lo + D_MODEL)).astype(BF16)


def _in_proj(x2, g, w_stack, layer, bd, hg, rope_p, rope_r, seq):
    rows = x2.shape[0]
    tm = IN_TM
    per_seq = seq // tm
    return pl.pallas_call(
        _in_proj_kernel,
        grid=(rows // tm,),
        in_specs=[
            pl.BlockSpec((tm, D_MODEL), lambda i: (i, 0)),
            _resident((1, D_MODEL)),
            _layer_resident(w_stack, layer),
            _resident((MXU_DIM, MXU_DIM)),
            _resident((1, hg.shape[1])),
            pl.BlockSpec((3, tm, LANES), lambda i: (0, i % per_seq, 0)),
            pl.BlockSpec((3, tm, LANES), lambda i: (0, i % per_seq, 0)),
        ],
        out_specs=[
            pl.BlockSpec((3, tm // dil, dil * LANES), lambda i: (0, i, 0)) for _, dil in A_GROUPS
        ] + [
            pl.BlockSpec((tm, N_MAIN), lambda i: (i, 0)),
            pl.BlockSpec((tm, N_GATE), lambda i: (i, 0)),
        ],
        out_shape=[
            jax.ShapeDtypeStruct((3, rows // dil, dil * LANES), BF16) for _, dil in A_GROUPS
        ] + [
            jax.ShapeDtypeStruct((rows, N_MAIN), BF16),
            jax.ShapeDtypeStruct((rows, N_GATE), BF16),
        ],
        scratch_shapes=[pltpu.VMEM((3 * (len(A_GROUPS) - 1), tm, LANES), F32)],
        compiler_params=_params("parallel"),
        name="in_proj",
    )(x2, g, w_stack, bd, hg, rope_p, rope_r)


def _dilated_kernel(a0_ref, a1_ref, a2_ref, o0_ref, l0_ref, o1_ref, l1_ref, o2_ref, l2_ref, *, seq):
    lane = lax.broadcasted_iota(jnp.int32, (1, LANES), 1)
    first = lane < HEAD_DIM
    for (_, dil), a_ref, o_ref, l_ref in zip(A_GROUPS, (a0_ref, a1_ref, a2_ref), (o0_ref, o1_ref, o2_ref),
                                             (l0_ref, l1_ref, l2_ref)):
        length = seq // dil
        tq = min(A_TQ, length)
        kw = min(tq + 2 * A_HALF, length)
        rel = (lax.broadcasted_iota(jnp.int32, (tq, kw), 0) - lax.broadcasted_iota(jnp.int32, (tq, kw), 1))
        ones = jnp.ones((kw, LANES), BF16)
        masks = {}
        for r in range(dil):
            lanes = slice(r * LANES, (r + 1) * LANES)
            for n0 in range(0, length, tq):
                k0 = min(max(n0 - A_HALF, 0), length - kw)
                if n0 - k0 not in masks:
                    masks[n0 - k0] = jnp.abs(rel + (n0 - k0)) <= A_HALF
                valid = masks[n0 - k0]
                q = a_ref[0, n0:n0 + tq, lanes]
                k = a_ref[1, k0:k0 + kw, lanes]
                v1 = jnp.concatenate([a_ref[2, k0:k0 + kw, lanes], ones], axis=1)
                outs, lses = [], []
                for head_lanes in (first, jnp.logical_not(first)):
                    qh = jnp.where(head_lanes, q, jnp.zeros_like(q))
                    s = lax.dot_general(qh, k, (((1,), (1,)), ((), ())), preferred_element_type=F32)
                    s = jnp.where(valid, s, NEG_INF)
                    m = jnp.max(s, axis=-1, keepdims=True)
                    e = jnp.exp(s - m).astype(BF16)
                    pv = jnp.dot(e, v1, preferred_element_type=F32)
                    den = pv[:, LANES:]
                    outs.append(pv[:, :LANES] / den)
                    lses.append(m + jnp.log(den))
                o_ref[n0:n0 + tq, lanes] = jnp.where(first, outs[0], outs[1])
                l_ref[n0:n0 + tq, lanes] = jnp.where(first, lses[0], lses[1])


def _dilated_attention(a_groups, batch, seq):
    in_specs, out_specs, out_shape = [], [], []
    for _, dil in A_GROUPS:
        length = seq // dil
        in_specs.append(pl.BlockSpec((3, length, dil * LANES), lambda b: (0, b, 0)))
        out_specs += [pl.BlockSpec((length, dil * LANES), lambda b: (b, 0))] * 2
        out_shape += [jax.ShapeDtypeStruct((batch * length, dil * LANES), F32)] * 2
    return pl.pallas_call(
        functools.partial(_dilated_kernel, seq=seq),
        grid=(batch,),
        in_specs=in_specs,
        out_specs=out_specs,
        out_shape=out_shape,
        compiler_params=_params("parallel"),
        name="dilated_attention",
    )(*a_groups)


def _diff_kernel(lam_ref, q_ref, k_ref, v_ref, g_ref, o_ref, v1_ref, *, lam_init, seq):
    lam_p = lam_ref[...]
    lam = (jnp.exp(jnp.sum(lam_p[0:1] * lam_p[1:2], axis=-1, keepdims=True))
           - jnp.exp(jnp.sum(lam_p[2:3] * lam_p[3:4], axis=-1, keepdims=True)) + lam_init)
    lane = lax.broadcasted_iota(jnp.int32, (1, LANES), 1)
    first = lane < HEAD_DIM
    k = k_ref[0]
    gain = g_ref[...] * (1.0 - lam_init)

    v1_ref[:, :LANES] = v_ref[0]
    v1_ref[:, LANES:] = jnp.ones((seq, LANES), BF16)

    def scores(i):
        q = q_ref[0, i * B_TQ:(i + 1) * B_TQ, :]
        zero = jnp.zeros_like(q)
        return [lax.dot_general(qm, k, (((1,), (1,)), ((), ())), preferred_element_type=F32)
                for qm in (jnp.where(first, q, zero), jnp.where(first, zero, q))]

    def softmax_pv(s):
        m = jnp.max(s, axis=-1, keepdims=True)
        e = jnp.exp2(s - m).astype(BF16)
        pv = jnp.dot(e, v1_ref[...], preferred_element_type=F32)
        return pv[:, :LANES] / pv[:, LANES:]

    n = seq // B_TQ
    ahead = scores(0)
    for i in range(n):
        cur = ahead
        if i + 1 < n:
            ahead = scores(i + 1)
        rows = slice(i * B_TQ, (i + 1) * B_TQ)
        o = softmax_pv(cur[0]) - lam * softmax_pv(cur[1])
        ms = jnp.mean(o * o, axis=-1, keepdims=True)
        o_ref[0, rows, :] = (o * lax.rsqrt(ms + NORM_EPS) * gain).astype(o_ref.dtype)


def _diff_attention(main3, lam_p, out_g, lam_init, batch, seq):
    qb, kb, vb = ((off - N_A) // LANES for off in (OFF_BQ, OFF_BK, OFF_BV))

    def head_block(col_block):
        return pl.BlockSpec((1, seq, LANES), lambda b, h: (b, 0, col_block + h))

    return pl.pallas_call(
        functools.partial(_diff_kernel, lam_init=lam_init, seq=seq),
        grid=(batch, B_HEADS),
        in_specs=[
            pl.BlockSpec((4, HEAD_DIM), lambda b, h: (0, 0)),
            head_block(qb), head_block(kb), head_block(vb),
            pl.BlockSpec((1, LANES), lambda b, h: (0, 0)),
        ],
        out_specs=head_block(0),
        out_shape=jax.ShapeDtypeStruct((batch, seq, B_WIDTH), BF16),
        scratch_shapes=[pltpu.VMEM((seq, 2 * LANES), BF16)],
        compiler_params=_params("parallel", "parallel"),
        name="diff_attention",
    )(lam_p, main3, main3, main3, out_g)


C_HEADS_PER_STEP = LANES // HEAD_DIM


def _retention_kernel(dec_ref, q_ref, k_ref, v_ref, cg_ref, g_ref, o_ref, *, seq):
    cc = RET_CHUNK
    n = seq // cc
    lane = lax.broadcasted_iota(jnp.int32, (1, LANES), 1)
    pos = lax.broadcasted_iota(jnp.int32, (cc, LANES), 0).astype(F32)
    rel = (lax.broadcasted_iota(jnp.int32, (cc, cc), 0) - lax.broadcasted_iota(jnp.int32, (cc, cc), 1)).astype(F32)
    g = g_ref[...]

    def rows(ref, i, cols=slice(None)):
        return ref[0, i * cc:(i + 1) * cc, cols]

    for hh in range(C_HEADS_PER_STEP):
        mine = (lane // HEAD_DIM) == hh
        cols = slice(hh * LANES, (hh + 1) * LANES)
        dec = dec_ref[hh]
        lg_f = -jnp.exp(dec[0:1, 0:1])
        lg_b = -jnp.exp(dec[1:2, 0:1])
        dq_f = jnp.exp(lg_f * (pos + 1.0)).astype(BF16)
        dk_f = jnp.exp(lg_f * (cc - 1.0 - pos)).astype(BF16)
        dq_b = jnp.exp(lg_b * (cc - pos)).astype(BF16)
        dk_b = jnp.exp(lg_b * pos).astype(BF16)
        chunk_f = jnp.exp(lg_f * cc)
        chunk_b = jnp.exp(lg_b * cc)
        decay = (jnp.where(rel >= 0, jnp.exp(lg_f * jnp.maximum(rel, 0.0)), 0.0)
                 + jnp.where(rel <= 0, jnp.exp(lg_b * jnp.maximum(-rel, 0.0)), 0.0))

        def kv_sum(i, dk, cols=cols):
            kd = rows(k_ref, i) * dk
            return lax.dot_general(kd, rows(v_ref, i, cols), (((0,), (0,)), ((), ())), preferred_element_type=F32)

        state = jnp.zeros((LANES, LANES), F32)
        fwd_states = []
        for i in range(n):
            fwd_states.append(state)
            if i + 1 < n:
                state = chunk_f * state + kv_sum(i, dk_f)
        state = jnp.zeros((LANES, LANES), F32)
        bwd_states = [None] * n
        for i in reversed(range(n)):
            bwd_states[i] = state
            if i > 0:
                state = chunk_b * state + kv_sum(i, dk_b)

        for i in range(n):
            q = rows(q_ref, i)
            q = jnp.where(mine, q, jnp.zeros_like(q))
            scores = lax.dot_general(q, rows(k_ref, i), (((1,), (1,)), ((), ())), preferred_element_type=F32)
            inner = jnp.dot((scores * decay).astype(BF16), rows(v_ref, i, cols), preferred_element_type=F32)
            q_cross = jnp.concatenate([q * dq_f, q * dq_b], axis=1)
            s_cross = jnp.concatenate([fwd_states[i], bwd_states[i]], axis=0).astype(BF16)
            out = inner + jnp.dot(q_cross, s_cross, preferred_element_type=F32)
            ms = jnp.mean(out * out, axis=-1, keepdims=True)
            y = out * lax.rsqrt(ms + NORM_EPS) * g
            o_ref[0, i * cc:(i + 1) * cc, cols] = (rows(cg_ref, i, cols).astype(F32) * y).astype(o_ref.dtype)


def _retention(main3, dec, out_g, batch, seq):
    hps = C_HEADS_PER_STEP
    qb, kb = ((off - N_A) // LANES for off in (OFF_CQ, OFF_CK))
    vb, gb = ((off - N_A) // (hps * LANES) for off in (OFF_CV, OFF_CG))
    return pl.pallas_call(
        functools.partial(_retention_kernel, seq=seq),
        grid=(batch, C_HEADS // hps),
        in_specs=[
            pl.BlockSpec((hps, 2, LANES), lambda b, j: (j, 0, 0)),
            pl.BlockSpec((1, seq, LANES), lambda b, j: (b, 0, qb + j)),
            pl.BlockSpec((1, seq, LANES), lambda b, j: (b, 0, kb + j)),
            pl.BlockSpec((1, seq, hps * LANES), lambda b, j: (b, 0, vb + j)),
            pl.BlockSpec((1, seq, hps * LANES), lambda b, j: (b, 0, gb + j)),
            pl.BlockSpec((1, LANES), lambda b, j: (0, 0)),
        ],
        out_specs=pl.BlockSpec((1, seq, hps * LANES), lambda b, j: (b, 0, j)),
        out_shape=jax.ShapeDtypeStruct((batch, seq, C_WIDTH), BF16),
        compiler_params=_params("parallel", "parallel"),
        name="retention",
    )(dec, main3, main3, main3, main3, out_g)


def _merge_mlp_kernel(x_ref, o0_ref, l0_ref, o1_ref, l1_ref, o2_ref, l2_ref, yb_ref, yc_ref, gate_ref,
                      wa_ref, wb_ref, wc_ref, wo_ref, g2_ref, w1_ref, w2_ref, o_ref, pos_ref):
    tm = x_ref.shape[0]
    os_, ls = [o0_ref[...]], [l0_ref[...]]
    slot = 0
    for (_, dil), o_in, l_in in zip(A_GROUPS[1:], (o1_ref, o2_ref), (l1_ref, l2_ref)):
        for src, dst in ((o_in, os_), (l_in, ls)):
            for r in range(dil):
                pos_ref[slot, pl.ds(r, tm // dil, stride=dil), :] = src[:, r * LANES:(r + 1) * LANES]
            dst.append(pos_ref[slot])
            slot += 1
    m = functools.reduce(jnp.maximum, ls)
    es = [jnp.exp(l - m) for l in ls]
    inv = 1.0 / functools.reduce(jnp.add, es)
    ya = jnp.concatenate([(o * (e * inv)).astype(BF16) for o, e in zip(os_, es)], axis=1)

    merged = gate_ref[:, 0:D_MODEL].astype(F32) * jnp.dot(ya, wa_ref[...], preferred_element_type=F32)
    merged = merged + gate_ref[:, D_MODEL:2 * D_MODEL].astype(F32) * jnp.dot(
        yb_ref[...], wb_ref[...], preferred_element_type=F32)
    merged = merged + gate_ref[:, 2 * D_MODEL:3 * D_MODEL].astype(F32) * jnp.dot(
        yc_ref[...], wc_ref[...], preferred_element_type=F32)
    x1 = x_ref[...] + jnp.dot(merged.astype(BF16), wo_ref[...], preferred_element_type=F32)

    ms = jnp.mean(x1 * x1, axis=-1, keepdims=True)
    xn = (x1 * lax.rsqrt(ms + NORM_EPS) * g2_ref[...]).astype(BF16)
    acc = x1
    for c in range(D_FF // D_MODEL):
        lo, hi = c * D_MODEL, (c + 1) * D_MODEL
        hdn = jnp.dot(xn, w1_ref[:, lo:hi].astype(BF16), preferred_element_type=F32)
        hdn = jnp.square(jnp.maximum(hdn, 0.0)).astype(BF16)
        acc = acc + jnp.dot(hdn, w2_ref[lo:hi, :], preferred_element_type=F32)
    o_ref[...] = acc


def _merge_mlp(x2, ya_parts, yb, yc, gates, wa, wb, wc, wo, g2, w1, w2, layer):
    rows = x2.shape[0]
    tm = OUT_TM

    def row_block(width):
        return pl.BlockSpec((tm, width), lambda i: (i, 0))

    a_specs = []
    for _, dil in A_GROUPS:
        a_specs += [pl.BlockSpec((tm // dil, dil * LANES), lambda i: (i, 0))] * 2

    return pl.pallas_call(
        _merge_mlp_kernel,
        grid=(rows // tm,),
        in_specs=[row_block(D_MODEL)] + a_specs + [
            row_block(B_WIDTH), row_block(C_WIDTH), row_block(N_GATE),
            _layer_resident(wa, layer), _layer_resident(wb, layer), _layer_resident(wc, layer),
            _layer_resident(wo, layer), _resident(g2.shape), _layer_resident(w1, layer), _layer_resident(w2, layer)],
        out_specs=row_block(D_MODEL),
        out_shape=jax.ShapeDtypeStruct((rows, D_MODEL), F32),
        scratch_shapes=[pltpu.VMEM((2 * (len(A_GROUPS) - 1), tm, LANES), F32)],
        compiler_params=_params("parallel"),
        name="merge_mlp",
    )(x2, *ya_parts, yb, yc, gates, wa, wb, wc, wo, g2, w1, w2)


def kernel(x, norm1_g, w_in, a_q_norm_g, a_k_norm_g, b_q_norm_g, b_k_norm_g, b_lambda_q1, b_lambda_k1, b_lambda_q2, b_lambda_k2, b_out_norm_g, c_decay_f, c_decay_b, c_out_norm_g, w_br_a, w_br_b, w_br_c, w_o, norm2_g, w_mlp1, w_mlp2):
    batch, seq, _ = x.shape
    depth = w_in.shape[0]
    rope_p = _rope_tables(seq, PART_ROT, ROPE_THETA, HEAD_DIM)
    rope_r = _rope_tables(seq, HEAD_DIM, RET_THETA, HEAD_DIM)
    head_id = np.arange(MXU_DIM) // HEAD_DIM
    bd = jnp.asarray(head_id[:, None] == head_id[None, :], BF16)

    w_br_a, w_br_b, w_br_c, w_o, w_mlp2 = (w.astype(BF16) for w in (w_br_a, w_br_b, w_br_c, w_o, w_mlp2))

    x2 = x.reshape(batch * seq, D_MODEL)
    for l in range(depth):
        a_heads = A_WIDTH // HEAD_DIM
        b_heads2 = B_WIDTH // HEAD_DIM
        head_gain = jnp.concatenate([
            jnp.tile(a_q_norm_g[l] * HEAD_DIM ** -0.5, a_heads), jnp.tile(a_k_norm_g[l], a_heads),
            jnp.tile(b_q_norm_g[l] * (HEAD_DIM ** -0.5 * math.log2(math.e)), b_heads2),
            jnp.tile(b_k_norm_g[l], b_heads2)])[None, :]
        a0, a1, a2, main, gates = _in_proj(
            x2, norm1_g[l][None, :], w_in, l, bd, head_gain, rope_p, rope_r, seq)
        main3 = main.reshape(batch, seq, N_MAIN)

        ya_parts = _dilated_attention((a0, a1, a2), batch, seq)

        lam_init = 0.8 - 0.6 * math.exp(-0.3 * l)
        lam_p = jnp.stack([b_lambda_q1[l], b_lambda_k1[l], b_lambda_q2[l], b_lambda_k2[l]], axis=0)
        yb = _diff_attention(main3, lam_p, b_out_norm_g[l][None, :], lam_init, batch, seq)

        dec = jnp.broadcast_to(jnp.stack([c_decay_f[l], c_decay_b[l]], axis=1)[:, :, None], (C_HEADS, 2, LANES))
        yc = _retention(main3, dec, c_out_norm_g[l][None, :], batch, seq)

        x2 = _merge_mlp(
            x2, ya_parts, yb.reshape(batch * seq, B_WIDTH), yc.reshape(batch * seq, C_WIDTH), gates,
            w_br_a, w_br_b, w_br_c, w_o, norm2_g[l][None, :], w_mlp1, w_mlp2, l)
    return x2.reshape(batch, seq, D_MODEL)
```

```python
import functools
import math

import jax
import jax.numpy as jnp
import numpy as np
from jax import lax
from jax.experimental import pallas as pl
from jax.experimental.pallas import tpu as pltpu

F32 = jnp.float32
BF16 = jnp.bfloat16

LANES = 128
MXU_DIM = 256
VMEM_LIMIT_BYTES = 56 * 1024 * 1024

D_MODEL = 1024
HEAD_DIM = 64
A_GROUPS = ((128, 1), (512, 4), (2048, 16))
A_WIDTH = 384
B_HEADS = 4
B_WIDTH = 512
C_HEADS = 4
C_QK_WIDTH = 256
C_WIDTH = 512
D_FF = 4096
N_BRANCHES = 3
ROPE_THETA = 500000.0
RET_THETA = 10000.0
PART_ROT = 16
NORM_EPS = 1e-6
NEG_INF = -1e30

OFF_AQ, OFF_AK, OFF_AV = 0, 384, 768
OFF_BQ, OFF_BK, OFF_BV = 1152, 1664, 2176
OFF_CQ, OFF_CK, OFF_CV, OFF_CG = 2688, 2944, 3200, 3712
OFF_GATE = 4224
N_A = OFF_BQ
N_MAIN = OFF_GATE - N_A
N_GATE = N_BRANCHES * D_MODEL
N_IN = OFF_GATE + N_GATE

IN_TM = 512
OUT_TM = 512
B_TQ = 256
B_QK_AHEAD = 3
RET_CHUNK = 256
A_TQ = 128
A_HALF = 64


def _params(*sem):
    return pltpu.CompilerParams(dimension_semantics=sem, vmem_limit_bytes=VMEM_LIMIT_BYTES)


def _resident(shape):
    nd = len(shape)
    return pl.BlockSpec(shape, lambda *_: (0,) * nd, pipeline_mode=pl.Buffered(1))


def _layer_resident(stacked, layer):
    _, rows, cols = stacked.shape
    return pl.BlockSpec((None, rows, cols), lambda *_: (layer, 0, 0), pipeline_mode=pl.Buffered(1))


def _rope_tables(seq, rot, theta, head_dim):
    half = rot // 2
    inv = 1.0 / (theta ** (jnp.arange(0, rot, 2, dtype=F32) / rot))
    ang = jnp.arange(seq, dtype=F32)[:, None] * inv[None, :]
    cos, sin = jnp.cos(ang), jnp.sin(ang)
    pad = head_dim - rot
    ones = jnp.ones((seq, pad), F32)
    zeros = jnp.zeros((seq, pad), F32)
    zh = jnp.zeros((seq, half), F32)
    c = jnp.concatenate([cos, cos, ones], axis=1)
    s_up = jnp.concatenate([zh, sin, zeros], axis=1)
    s_dn = jnp.concatenate([-sin, zh, zeros], axis=1)
    reps = LANES // head_dim
    return jnp.stack([jnp.tile(t, (1, reps)) for t in (c, s_up, s_dn)], axis=0)


def _apply_rope(t, tab, half):
    return (t * tab[0]
            + pltpu.roll(t, half, 1) * tab[1]
            + pltpu.roll(t, LANES - half, 1) * tab[2])


def _in_proj_kernel(x_ref, g_ref, w_ref, bd_ref, hg_ref, rp_ref, rr_ref,
                    a0_ref, a1_ref, a2_ref, main_ref, gate_ref, stage_ref):
    a_refs = (a0_ref, a1_ref, a2_ref)
    n_groups = len(A_GROUPS)
    tm = x_ref.shape[0]
    x = x_ref[...]
    ms = jnp.mean(x * x, axis=-1, keepdims=True)
    xn = (x * lax.rsqrt(ms + NORM_EPS) * g_ref[...]).astype(BF16)
    bd = bd_ref[...]
    rp = rp_ref[...]
    rr = rr_ref[...]

    def proj(lo, hi):
        return jnp.dot(xn, w_ref[:, lo:hi].astype(BF16), preferred_element_type=F32)

    def store_a(slab, t):
        part, g = divmod(slab, n_groups)
        dil = A_GROUPS[g][1]
        if dil == 1:
            a_refs[g][part] = t.astype(BF16)
            return
        stage = stage_ref.at[part * (n_groups - 1) + g - 1]
        stage[...] = t
        for r in range(dil):
            a_refs[g][part, :, r * LANES:(r + 1) * LANES] = stage[pl.ds(r, tm // dil, stride=dil), :].astype(BF16)

    def store(lo, t):
        hi = lo + t.shape[1]
        if hi <= N_A:
            for c in range(t.shape[1] // LANES):
                store_a(lo // LANES + c, t[:, c * LANES:(c + 1) * LANES])
        else:
            main_ref[:, lo - N_A:hi - N_A] = t.astype(BF16)

    def store_roped(t, lo, tab, half, scale=None):
        for c in range(t.shape[1] // LANES):
            o = _apply_rope(t[:, c * LANES:(c + 1) * LANES], tab, half)
            if scale is not None:
                o = o * scale
            store(lo + c * LANES, o)

    hg_off = 0
    for seg_lo, seg_hi in ((OFF_AQ, OFF_AV), (OFF_BQ, OFF_BV)):
        seg = proj(seg_lo, seg_hi)
        for lo in range(0, seg_hi - seg_lo, MXU_DIM):
            t = seg[:, lo:lo + MXU_DIM]
            ss = jnp.dot((t * t).astype(BF16), bd, preferred_element_type=F32)
            t = t * lax.rsqrt(ss * (1.0 / HEAD_DIM) + NORM_EPS) * hg_ref[:, hg_off:hg_off + MXU_DIM]
            store_roped(t, seg_lo + lo, rp, PART_ROT // 2)
            hg_off += MXU_DIM

    seg = proj(OFF_CQ, OFF_CV)
    store_roped(seg[:, :C_QK_WIDTH], OFF_CQ, rr, HEAD_DIM // 2)
    store_roped(seg[:, C_QK_WIDTH:], OFF_CK, rr, HEAD_DIM // 2, scale=HEAD_DIM ** -0.5)

    for lo, hi in ((OFF_AV, OFF_BQ), (OFF_BV, OFF_CQ), (OFF_CV, OFF_CG)):
        store(lo, proj(lo, hi))
    store(OFF_CG, jax.nn.silu(proj(OFF_CG, OFF_GATE)))

    for j in range(N_BRANCHES):
        lo = OFF_GATE + j * D_MODEL
        gate_ref[:, j * D_MODEL:(j + 1) * D_MODEL] = jax.nn.sigmoid(proj(lo, lo + D_MODEL)).astype(BF16)


def _in_proj(x2, g, w_stack, layer, bd, hg, rope_p, rope_r, seq):
    rows = x2.shape[0]
    tm = IN_TM
    per_seq = seq // tm
    return pl.pallas_call(
        _in_proj_kernel,
        grid=(rows // tm,),
        in_specs=[
            pl.BlockSpec((tm, D_MODEL), lambda i: (i, 0)),
            _resident((1, D_MODEL)),
            _layer_resident(w_stack, layer),
            _resident((MXU_DIM, MXU_DIM)),
            _resident((1, hg.shape[1])),
            pl.BlockSpec((3, tm, LANES), lambda i: (0, i % per_seq, 0)),
            pl.BlockSpec((3, tm, LANES), lambda i: (0, i % per_seq, 0)),
        ],
        out_specs=[
            pl.BlockSpec((3, tm // dil, dil * LANES), lambda i: (0, i, 0)) for _, dil in A_GROUPS
        ] + [
            pl.BlockSpec((tm, N_MAIN), lambda i: (i, 0)),
            pl.BlockSpec((tm, N_GATE), lambda i: (i, 0)),
        ],
        out_shape=[
            jax.ShapeDtypeStruct((3, rows // dil, dil * LANES), BF16) for _, dil in A_GROUPS
        ] + [
            jax.ShapeDtypeStruct((rows, N_MAIN), BF16),
            jax.ShapeDtypeStruct((rows, N_GATE), BF16),
        ],
        scratch_shapes=[pltpu.VMEM((3 * (len(A_GROUPS) - 1), tm, LANES), F32)],
        compiler_params=_params("parallel"),
        name="in_proj",
    )(x2, g, w_stack, bd, hg, rope_p, rope_r)


def _dilated_kernel(a0_ref, a1_ref, a2_ref, o0_ref, l0_ref, o1_ref, l1_ref, o2_ref, l2_ref, *, seq):
    lane = lax.broadcasted_iota(jnp.int32, (1, LANES), 1)
    first = lane < HEAD_DIM
    for (_, dil), a_ref, o_ref, l_ref in zip(A_GROUPS, (a0_ref, a1_ref, a2_ref), (o0_ref, o1_ref, o2_ref),
                                             (l0_ref, l1_ref, l2_ref)):
        length = seq // dil
        tq = min(A_TQ, length)
        kw = min(tq + 2 * A_HALF, length)
        rel = (lax.broadcasted_iota(jnp.int32, (tq, kw), 0) - lax.broadcasted_iota(jnp.int32, (tq, kw), 1))
        ones = jnp.ones((kw, LANES), BF16)
        masks = {}
        for r in range(dil):
            lanes = slice(r * LANES, (r + 1) * LANES)
            for n0 in range(0, length, tq):
                k0 = min(max(n0 - A_HALF, 0), length - kw)
                if n0 - k0 not in masks:
                    masks[n0 - k0] = jnp.abs(rel + (n0 - k0)) <= A_HALF
                valid = masks[n0 - k0]
                q = a_ref[0, n0:n0 + tq, lanes]
                k = a_ref[1, k0:k0 + kw, lanes]
                v1 = jnp.concatenate([a_ref[2, k0:k0 + kw, lanes], ones], axis=1)
                outs, lses = [], []
                for head_lanes in (first, jnp.logical_not(first)):
                    qh = jnp.where(head_lanes, q, jnp.zeros_like(q))
                    s = lax.dot_general(qh, k, (((1,), (1,)), ((), ())), preferred_element_type=F32)
                    s = jnp.where(valid, s, NEG_INF)
                    m = jnp.max(s, axis=-1, keepdims=True)
                    e = jnp.exp(s - m).astype(BF16)
                    pv = jnp.dot(e, v1, preferred_element_type=F32)
                    den = pv[:, LANES:]
                    outs.append(pv[:, :LANES] / den)
                    lses.append(m + jnp.log(den))
                o_ref[n0:n0 + tq, lanes] = jnp.where(first, outs[0], outs[1])
                l_ref[n0:n0 + tq, lanes] = jnp.where(first, lses[0], lses[1])


def _dilated_attention(a_groups, batch, seq):
    in_specs, out_specs, out_shape = [], [], []
    for _, dil in A_GROUPS:
        length = seq // dil
        in_specs.append(pl.BlockSpec((3, length, dil * LANES), lambda b: (0, b, 0)))
        out_specs += [pl.BlockSpec((length, dil * LANES), lambda b: (b, 0))] * 2
        out_shape += [jax.ShapeDtypeStruct((batch * length, dil * LANES), F32)] * 2
    return pl.pallas_call(
        functools.partial(_dilated_kernel, seq=seq),
        grid=(batch,),
        in_specs=in_specs,
        out_specs=out_specs,
        out_shape=out_shape,
        compiler_params=_params("parallel"),
        name="dilated_attention",
    )(*a_groups)


def _diff_kernel(lam_ref, q_ref, k_ref, v_ref, g_ref, o_ref, v1_ref, *, lam_init, seq):
    lam_p = lam_ref[...]
    lam = (jnp.exp(jnp.sum(lam_p[0:1] * lam_p[1:2], axis=-1, keepdims=True))
           - jnp.exp(jnp.sum(lam_p[2:3] * lam_p[3:4], axis=-1, keepdims=True)) + lam_init)
    lane = lax.broadcasted_iota(jnp.int32, (1, LANES), 1)
    first = lane < HEAD_DIM
    k = k_ref[0]
    gain = g_ref[...] * (1.0 - lam_init)

    v1_ref[:, :LANES] = v_ref[0]
    v1_ref[:, LANES:] = jnp.ones((seq, LANES), BF16)

    def score(unit):
        i, second_map = divmod(unit, 2)
        q = q_ref[0, i * B_TQ:(i + 1) * B_TQ, :]
        zero = jnp.zeros_like(q)
        qm = jnp.where(first, zero, q) if second_map else jnp.where(first, q, zero)
        return lax.dot_general(qm, k, (((1,), (1,)), ((), ())), preferred_element_type=F32)

    def softmax_pv(s):
        m = jnp.max(s, axis=-1, keepdims=True)
        e = jnp.exp2(s - m).astype(BF16)
        pv = jnp.dot(e, v1_ref[...], preferred_element_type=F32)
        return pv[:, :LANES] / pv[:, LANES:]

    n_units = 2 * (seq // B_TQ)
    pending = [score(u) for u in range(B_QK_AHEAD)]
    outs = []
    for u in range(n_units):
        cur = pending.pop(0)
        if u + B_QK_AHEAD < n_units:
            pending.append(score(u + B_QK_AHEAD))
        outs.append(softmax_pv(cur))
        if u % 2 == 1:
            rows = slice((u // 2) * B_TQ, (u // 2 + 1) * B_TQ)
            o = outs[-2] - lam * outs[-1]
            ms = jnp.mean(o * o, axis=-1, keepdims=True)
            o_ref[0, rows, :] = (o * lax.rsqrt(ms + NORM_EPS) * gain).astype(o_ref.dtype)


def _diff_attention(main3, lam_p, out_g, lam_init, batch, seq):
    qb, kb, vb = ((off - N_A) // LANES for off in (OFF_BQ, OFF_BK, OFF_BV))

    def head_block(col_block):
        return pl.BlockSpec((1, seq, LANES), lambda b, h: (b, 0, col_block + h))

    return pl.pallas_call(
        functools.partial(_diff_kernel, lam_init=lam_init, seq=seq),
        grid=(batch, B_HEADS),
        in_specs=[
            pl.BlockSpec((4, HEAD_DIM), lambda b, h: (0, 0)),
            head_block(qb), head_block(kb), head_block(vb),
            pl.BlockSpec((1, LANES), lambda b, h: (0, 0)),
        ],
        out_specs=head_block(0),
        out_shape=jax.ShapeDtypeStruct((batch, seq, B_WIDTH), BF16),
        scratch_shapes=[pltpu.VMEM((seq, 2 * LANES), BF16)],
        compiler_params=_params("parallel", "parallel"),
        name="diff_attention",
    )(lam_p, main3, main3, main3, out_g)


C_HEADS_PER_STEP = LANES // HEAD_DIM


def _retention_kernel(dec_ref, q_ref, k_ref, v_ref, cg_ref, g_ref, o_ref, *, seq):
    cc = RET_CHUNK
    n = seq // cc
    lane = lax.broadcasted_iota(jnp.int32, (1, LANES), 1)
    pos = lax.broadcasted_iota(jnp.int32, (cc, LANES), 0).astype(F32)
    rel = (lax.broadcasted_iota(jnp.int32, (cc, cc), 0) - lax.broadcasted_iota(jnp.int32, (cc, cc), 1)).astype(F32)
    g = g_ref[...]

    def rows(ref, i, cols=slice(None)):
        return ref[0, i * cc:(i + 1) * cc, cols]

    def head(hh):
        mine = (lane // HEAD_DIM) == hh
        cols = slice(hh * LANES, (hh + 1) * LANES)
        dec = dec_ref[hh]
        lg_f = -jnp.exp(dec[0:1, 0:1])
        lg_b = -jnp.exp(dec[1:2, 0:1])
        dq_f = jnp.exp(lg_f * (pos + 1.0)).astype(BF16)
        dk_f = jnp.exp(lg_f * (cc - 1.0 - pos)).astype(BF16)
        dq_b = jnp.exp(lg_b * (cc - pos)).astype(BF16)
        dk_b = jnp.exp(lg_b * pos).astype(BF16)
        chunk_f = jnp.exp(lg_f * cc)
        chunk_b = jnp.exp(lg_b * cc)
        decay = (jnp.where(rel >= 0, jnp.exp(lg_f * jnp.maximum(rel, 0.0)), 0.0)
                 + jnp.where(rel <= 0, jnp.exp(lg_b * jnp.maximum(-rel, 0.0)), 0.0))

        def kv_sum(i, dk):
            kd = rows(k_ref, i) * dk
            return lax.dot_general(kd, rows(v_ref, i, cols), (((0,), (0,)), ((), ())), preferred_element_type=F32)

        state = jnp.zeros((LANES, LANES), F32)
        fwd_states = []
        for i in range(n):
            fwd_states.append(state)
            if i + 1 < n:
                state = chunk_f * state + kv_sum(i, dk_f)
        state = jnp.zeros((LANES, LANES), F32)
        bwd_states = [None] * n
        for i in reversed(range(n)):
            bwd_states[i] = state
            if i > 0:
                state = chunk_b * state + kv_sum(i, dk_b)

        def qk(i):
            q = rows(q_ref, i)
            q = jnp.where(mine, q, jnp.zeros_like(q))
            return q, lax.dot_general(q, rows(k_ref, i), (((1,), (1,)), ((), ())), preferred_element_type=F32)

        def finish(i, q, scores):
            inner = jnp.dot((scores * decay).astype(BF16), rows(v_ref, i, cols), preferred_element_type=F32)
            q_cross = jnp.concatenate([q * dq_f, q * dq_b], axis=1)
            s_cross = jnp.concatenate([fwd_states[i], bwd_states[i]], axis=0).astype(BF16)
            out = inner + jnp.dot(q_cross, s_cross, preferred_element_type=F32)
            ms = jnp.mean(out * out, axis=-1, keepdims=True)
            y = out * lax.rsqrt(ms + NORM_EPS) * g
            o_ref[0, i * cc:(i + 1) * cc, cols] = (rows(cg_ref, i, cols).astype(F32) * y).astype(o_ref.dtype)

        return qk, finish

    heads = [head(hh) for hh in range(C_HEADS_PER_STEP)]
    units = [(hh, i) for i in range(n) for hh in range(C_HEADS_PER_STEP)]
    nxt = heads[units[0][0]][0](units[0][1])
    for u, (hh, i) in enumerate(units):
        q, scores = nxt
        if u + 1 < len(units):
            nxt = heads[units[u + 1][0]][0](units[u + 1][1])
        heads[hh][1](i, q, scores)


def _retention(main3, dec, out_g, batch, seq):
    hps = C_HEADS_PER_STEP
    qb, kb = ((off - N_A) // LANES for off in (OFF_CQ, OFF_CK))
    vb, gb = ((off - N_A) // (hps * LANES) for off in (OFF_CV, OFF_CG))
    return pl.pallas_call(
        functools.partial(_retention_kernel, seq=seq),
        grid=(batch, C_HEADS // hps),
        in_specs=[
            pl.BlockSpec((hps, 2, LANES), lambda b, j: (j, 0, 0)),
            pl.BlockSpec((1, seq, LANES), lambda b, j: (b, 0, qb + j)),
            pl.BlockSpec((1, seq, LANES), lambda b, j: (b, 0, kb + j)),
            pl.BlockSpec((1, seq, hps * LANES), lambda b, j: (b, 0, vb + j)),
            pl.BlockSpec((1, seq, hps * LANES), lambda b, j: (b, 0, gb + j)),
            pl.BlockSpec((1, LANES), lambda b, j: (0, 0)),
        ],
        out_specs=pl.BlockSpec((1, seq, hps * LANES), lambda b, j: (b, 0, j)),
        out_shape=jax.ShapeDtypeStruct((batch, seq, C_WIDTH), BF16),
        compiler_params=_params("parallel", "parallel"),
        name="retention",
    )(dec, main3, main3, main3, main3, out_g)


def _merge_mlp_kernel(x_ref, o0_ref, l0_ref, o1_ref, l1_ref, o2_ref, l2_ref, yb_ref, yc_ref, gate_ref,
                      wa_ref, wb_ref, wc_ref, wo_ref, g2_ref, w1_ref, w2_ref, o_ref, pos_ref):
    tm = x_ref.shape[0]
    os_, ls = [o0_ref[...]], [l0_ref[...]]
    slot = 0
    for (_, dil), o_in, l_in in zip(A_GROUPS[1:], (o1_ref, o2_ref), (l1_ref, l2_ref)):
        for src, dst in ((o_in, os_), (l_in, ls)):
            for r in range(dil):
                pos_ref[slot, pl.ds(r, tm // dil, stride=dil), :] = src[:, r * LANES:(r + 1) * LANES]
            dst.append(pos_ref[slot])
            slot += 1
    m = functools.reduce(jnp.maximum, ls)
    es = [jnp.exp(l - m) for l in ls]
    inv = 1.0 / functools.reduce(jnp.add, es)
    ya = jnp.concatenate([(o * (e * inv)).astype(BF16) for o, e in zip(os_, es)], axis=1)

    merged = gate_ref[:, 0:D_MODEL].astype(F32) * jnp.dot(ya, wa_ref[...], preferred_element_type=F32)
    merged = merged + gate_ref[:, D_MODEL:2 * D_MODEL].astype(F32) * jnp.dot(
        yb_ref[...], wb_ref[...], preferred_element_type=F32)
    merged = merged + gate_ref[:, 2 * D_MODEL:3 * D_MODEL].astype(F32) * jnp.dot(
        yc_ref[...], wc_ref[...], preferred_element_type=F32)
    x1 = x_ref[...] + jnp.dot(merged.astype(BF16), wo_ref[...], preferred_element_type=F32)

    ms = jnp.mean(x1 * x1, axis=-1, keepdims=True)
    xn = (x1 * lax.rsqrt(ms + NORM_EPS) * g2_ref[...]).astype(BF16)
    acc = x1
    for c in range(D_FF // D_MODEL):
        lo, hi = c * D_MODEL, (c + 1) * D_MODEL
        hdn = jnp.dot(xn, w1_ref[:, lo:hi].astype(BF16), preferred_element_type=F32)
        hdn = jnp.square(jnp.maximum(hdn, 0.0)).astype(BF16)
        acc = acc + jnp.dot(hdn, w2_ref[lo:hi, :], preferred_element_type=F32)
    o_ref[...] = acc


def _merge_mlp(x2, ya_parts, yb, yc, gates, wa, wb, wc, wo, g2, w1, w2, layer):
    rows = x2.shape[0]
    tm = OUT_TM

    def row_block(width):
        return pl.BlockSpec((tm, width), lambda i: (i, 0))

    a_specs = []
    for _, dil in A_GROUPS:
        a_specs += [pl.BlockSpec((tm // dil, dil * LANES), lambda i: (i, 0))] * 2

    return pl.pallas_call(
        _merge_mlp_kernel,
        grid=(rows // tm,),
        in_specs=[row_block(D_MODEL)] + a_specs + [
            row_block(B_WIDTH), row_block(C_WIDTH), row_block(N_GATE),
            _layer_resident(wa, layer), _layer_resident(wb, layer), _layer_resident(wc, layer),
            _layer_resident(wo, layer), _resident(g2.shape), _layer_resident(w1, layer), _layer_resident(w2, layer)],
        out_specs=row_block(D_MODEL),
        out_shape=jax.ShapeDtypeStruct((rows, D_MODEL), F32),
        scratch_shapes=[pltpu.VMEM((2 * (len(A_GROUPS) - 1), tm, LANES), F32)],
        compiler_params=_params("parallel"),
        name="merge_mlp",
    )(x2, *ya_parts, yb, yc, gates, wa, wb, wc, wo, g2, w1, w2)


def kernel(x, norm1_g, w_in, a_q_norm_g, a_k_norm_g, b_q_norm_g, b_k_norm_g, b_lambda_q1, b_lambda_k1, b_lambda_q2, b_lambda_k2, b_out_norm_g, c_decay_f, c_decay_b, c_out_norm_g, w_br_a, w_br_b, w_br_c, w_o, norm2_g, w_mlp1, w_mlp2):
    batch, seq, _ = x.shape
    depth = w_in.shape[0]
    rope_p = _rope_tables(seq, PART_ROT, ROPE_THETA, HEAD_DIM)
    rope_r = _rope_tables(seq, HEAD_DIM, RET_THETA, HEAD_DIM)
    head_id = np.arange(MXU_DIM) // HEAD_DIM
    bd = jnp.asarray(head_id[:, None] == head_id[None, :], BF16)

    w_br_a, w_br_b, w_br_c, w_o, w_mlp2 = (w.astype(BF16) for w in (w_br_a, w_br_b, w_br_c, w_o, w_mlp2))

    x2 = x.reshape(batch * seq, D_MODEL)
    for l in range(depth):
        a_heads = A_WIDTH // HEAD_DIM
        b_heads2 = B_WIDTH // HEAD_DIM
        head_gain = jnp.concatenate([
            jnp.tile(a_q_norm_g[l] * HEAD_DIM ** -0.5, a_heads), jnp.tile(a_k_norm_g[l], a_heads),
            jnp.tile(b_q_norm_g[l] * (HEAD_DIM ** -0.5 * math.log2(math.e)), b_heads2),
            jnp.tile(b_k_norm_g[l], b_heads2)])[None, :]
        a0, a1, a2, main, gates = _in_proj(
            x2, norm1_g[l][None, :], w_in, l, bd, head_gain, rope_p, rope_r, seq)
        main3 = main.reshape(batch, seq, N_MAIN)

        ya_parts = _dilated_attention((a0, a1, a2), batch, seq)

        lam_init = 0.8 - 0.6 * math.exp(-0.3 * l)
        lam_p = jnp.stack([b_lambda_q1[l], b_lambda_k1[l], b_lambda_q2[l], b_lambda_k2[l]], axis=0)
        yb = _diff_attention(main3, lam_p, b_out_norm_g[l][None, :], lam_init, batch, seq)

        dec = jnp.broadcast_to(jnp.stack([c_decay_f[l], c_decay_b[l]], axis=1)[:, :, None], (C_HEADS, 2, LANES))
        yc = _retention(main3, dec, c_out_norm_g[l][None, :], batch, seq)

        x2 = _merge_mlp(
            x2, ya_parts, yb.reshape(batch * seq, B_WIDTH), yc.reshape(batch * seq, C_WIDTH), gates,
            w_br_a, w_br_b, w_br_c, w_o, norm2_g[l][None, :], w_mlp1, w_mlp2, l)
    return x2.reshape(batch, seq, D_MODEL)
```

```python
import functools
import math

import jax
import jax.numpy as jnp
import numpy as np
from jax import lax
from jax.experimental import pallas as pl
from jax.experimental.pallas import tpu as pltpu

F32 = jnp.float32
BF16 = jnp.bfloat16

LANES = 128
MXU_DIM = 256
VMEM_LIMIT_BYTES = 56 * 1024 * 1024

D_MODEL = 1024
HEAD_DIM = 64
A_GROUPS = ((128, 1), (512, 4), (2048, 16))
A_WIDTH = 384
B_HEADS = 4
B_WIDTH = 512
C_HEADS = 4
C_QK_WIDTH = 256
C_WIDTH = 512
D_FF = 4096
N_BRANCHES = 3
ROPE_THETA = 500000.0
RET_THETA = 10000.0
PART_ROT = 16
NORM_EPS = 1e-6
NEG_INF = -1e30

OFF_AQ, OFF_AK, OFF_AV = 0, 384, 768
OFF_BQ, OFF_BK, OFF_BV = 1152, 1664, 2176
OFF_CQ, OFF_CK, OFF_CV, OFF_CG = 2688, 2944, 3200, 3712
OFF_GATE = 4224
N_A = OFF_BQ
N_MAIN = OFF_GATE - N_A
N_GATE = N_BRANCHES * D_MODEL
N_IN = OFF_GATE + N_GATE

IN_TM = 512
OUT_TM = 512
B_TQ = 256
B_QK_AHEAD = 3
RET_CHUNK = 256
A_TQ = 128
A_HALF = 64
assert all(window // (2 * dilation) == A_HALF for window, dilation in A_GROUPS)


def _params(*sem):
    return pltpu.CompilerParams(dimension_semantics=sem, vmem_limit_bytes=VMEM_LIMIT_BYTES)


def _resident(shape):
    nd = len(shape)
    return pl.BlockSpec(shape, lambda *_: (0,) * nd, pipeline_mode=pl.Buffered(1))


def _layer_resident(stacked, layer):
    _, rows, cols = stacked.shape
    return pl.BlockSpec((None, rows, cols), lambda *_: (layer, 0, 0), pipeline_mode=pl.Buffered(1))


def _rope_tables(seq, rot, theta, head_dim):
    half = rot // 2
    inv = 1.0 / (theta ** (jnp.arange(0, rot, 2, dtype=F32) / rot))
    ang = jnp.arange(seq, dtype=F32)[:, None] * inv[None, :]
    cos, sin = jnp.cos(ang), jnp.sin(ang)
    pad = head_dim - rot
    ones = jnp.ones((seq, pad), F32)
    zeros = jnp.zeros((seq, pad), F32)
    zh = jnp.zeros((seq, half), F32)
    c = jnp.concatenate([cos, cos, ones], axis=1)
    s_up = jnp.concatenate([zh, sin, zeros], axis=1)
    s_dn = jnp.concatenate([-sin, zh, zeros], axis=1)
    reps = LANES // head_dim
    return jnp.stack([jnp.tile(t, (1, reps)) for t in (c, s_up, s_dn)], axis=0)


def _apply_rope(t, tab, half):
    return (t * tab[0]
            + pltpu.roll(t, half, 1) * tab[1]
            + pltpu.roll(t, LANES - half, 1) * tab[2])


def _in_proj_kernel(x_ref, g_ref, w_ref, bd_ref, hg_ref, rp_ref, rr_ref,
                    a0_ref, a1_ref, a2_ref, main_ref, gate_ref, stage_ref):
    a_refs = (a0_ref, a1_ref, a2_ref)
    n_groups = len(A_GROUPS)
    tm = x_ref.shape[0]
    x = x_ref[...]
    ms = jnp.mean(x * x, axis=-1, keepdims=True)
    xn = (x * lax.rsqrt(ms + NORM_EPS) * g_ref[...]).astype(BF16)
    bd = bd_ref[...]
    rp = rp_ref[...]
    rr = rr_ref[...]

    def proj(lo, hi):
        return jnp.dot(xn, w_ref[:, lo:hi].astype(BF16), preferred_element_type=F32)

    def store_a(slab, t):
        part, g = divmod(slab, n_groups)
        dil = A_GROUPS[g][1]
        if dil == 1:
            a_refs[g][part] = t.astype(BF16)
            return
        stage = stage_ref.at[part * (n_groups - 1) + g - 1]
        stage[...] = t
        for r in range(dil):
            a_refs[g][part, :, r * LANES:(r + 1) * LANES] = stage[pl.ds(r, tm // dil, stride=dil), :].astype(BF16)

    def store(lo, t):
        hi = lo + t.shape[1]
        if hi <= N_A:
            for c in range(t.shape[1] // LANES):
                store_a(lo // LANES + c, t[:, c * LANES:(c + 1) * LANES])
        else:
            main_ref[:, lo - N_A:hi - N_A] = t.astype(BF16)

    def store_roped(t, lo, tab, half, scale=None):
        for c in range(t.shape[1] // LANES):
            o = _apply_rope(t[:, c * LANES:(c + 1) * LANES], tab, half)
            if scale is not None:
                o = o * scale
            store(lo + c * LANES, o)

    hg_off = 0
    for seg_lo, seg_hi in ((OFF_AQ, OFF_AV), (OFF_BQ, OFF_BV)):
        seg = proj(seg_lo, seg_hi)
        for lo in range(0, seg_hi - seg_lo, MXU_DIM):
            t = seg[:, lo:lo + MXU_DIM]
            ss = jnp.dot((t * t).astype(BF16), bd, preferred_element_type=F32)
            t = t * lax.rsqrt(ss * (1.0 / HEAD_DIM) + NORM_EPS) * hg_ref[:, hg_off:hg_off + MXU_DIM]
            store_roped(t, seg_lo + lo, rp, PART_ROT // 2)
            hg_off += MXU_DIM

    seg = proj(OFF_CQ, OFF_CV)
    store_roped(seg[:, :C_QK_WIDTH], OFF_CQ, rr, HEAD_DIM // 2)
    store_roped(seg[:, C_QK_WIDTH:], OFF_CK, rr, HEAD_DIM // 2, scale=HEAD_DIM ** -0.5)

    for lo, hi in ((OFF_AV, OFF_BQ), (OFF_BV, OFF_CQ), (OFF_CV, OFF_CG)):
        store(lo, proj(lo, hi))
    store(OFF_CG, jax.nn.silu(proj(OFF_CG, OFF_GATE)))

    for j in range(N_BRANCHES):
        lo = OFF_GATE + j * D_MODEL
        gate_ref[:, j * D_MODEL:(j + 1) * D_MODEL] = jax.nn.sigmoid(proj(lo, lo + D_MODEL)).astype(BF16)


def _in_proj(x2, g, w_stack, layer, bd, hg, rope_p, rope_r, seq):
    rows = x2.shape[0]
    tm = IN_TM
    per_seq = seq // tm
    return pl.pallas_call(
        _in_proj_kernel,
        grid=(rows // tm,),
        in_specs=[
            pl.BlockSpec((tm, D_MODEL), lambda i: (i, 0)),
            _resident((1, D_MODEL)),
            _layer_resident(w_stack, layer),
            _resident((MXU_DIM, MXU_DIM)),
            _resident((1, hg.shape[1])),
            pl.BlockSpec((3, tm, LANES), lambda i: (0, i % per_seq, 0)),
            pl.BlockSpec((3, tm, LANES), lambda i: (0, i % per_seq, 0)),
        ],
        out_specs=[
            pl.BlockSpec((3, tm // dil, dil * LANES), lambda i: (0, i, 0)) for _, dil in A_GROUPS
        ] + [
            pl.BlockSpec((tm, N_MAIN), lambda i: (i, 0)),
            pl.BlockSpec((tm, N_GATE), lambda i: (i, 0)),
        ],
        out_shape=[
            jax.ShapeDtypeStruct((3, rows // dil, dil * LANES), BF16) for _, dil in A_GROUPS
        ] + [
            jax.ShapeDtypeStruct((rows, N_MAIN), BF16),
            jax.ShapeDtypeStruct((rows, N_GATE), BF16),
        ],
        scratch_shapes=[pltpu.VMEM((3 * (len(A_GROUPS) - 1), tm, LANES), F32)],
        compiler_params=_params("parallel"),
        name="in_proj",
    )(x2, g, w_stack, bd, hg, rope_p, rope_r)


def _dilated_kernel(a0_ref, a1_ref, a2_ref, o0_ref, l0_ref, o1_ref, l1_ref, o2_ref, l2_ref, *, seq):
    lane = lax.broadcasted_iota(jnp.int32, (1, LANES), 1)
    first = lane < HEAD_DIM
    for (_, dil), a_ref, o_ref, l_ref in zip(A_GROUPS, (a0_ref, a1_ref, a2_ref), (o0_ref, o1_ref, o2_ref),
                                             (l0_ref, l1_ref, l2_ref)):
        length = seq // dil
        tq = min(A_TQ, length)
        kw = min(tq + 2 * A_HALF, length)
        rel = (lax.broadcasted_iota(jnp.int32, (tq, kw), 0) - lax.broadcasted_iota(jnp.int32, (tq, kw), 1))
        ones = jnp.ones((kw, LANES), BF16)
        masks = {}
        for r in range(dil):
            lanes = slice(r * LANES, (r + 1) * LANES)
            for n0 in range(0, length, tq):
                k0 = min(max(n0 - A_HALF, 0), length - kw)
                if n0 - k0 not in masks:
                    masks[n0 - k0] = jnp.abs(rel + (n0 - k0)) <= A_HALF
                valid = masks[n0 - k0]
                q = a_ref[0, n0:n0 + tq, lanes]
                k = a_ref[1, k0:k0 + kw, lanes]
                v1 = jnp.concatenate([a_ref[2, k0:k0 + kw, lanes], ones], axis=1)
                outs, lses = [], []
                for head_lanes in (first, jnp.logical_not(first)):
                    qh = jnp.where(head_lanes, q, jnp.zeros_like(q))
                    s = lax.dot_general(qh, k, (((1,), (1,)), ((), ())), preferred_element_type=F32)
                    s = jnp.where(valid, s, NEG_INF)
                    m = jnp.max(s, axis=-1, keepdims=True)
                    e = jnp.exp2(s - m).astype(BF16)
                    pv = jnp.dot(e, v1, preferred_element_type=F32)
                    den = pv[:, LANES:]
                    outs.append(pv[:, :LANES] / den)
                    lses.append(m + jnp.log2(den))
                o_ref[n0:n0 + tq, lanes] = jnp.where(first, outs[0], outs[1])
                l_ref[n0:n0 + tq, lanes] = jnp.where(first, lses[0], lses[1])


def _dilated_attention(a_groups, batch, seq):
    in_specs, out_specs, out_shape = [], [], []
    for _, dil in A_GROUPS:
        length = seq // dil
        in_specs.append(pl.BlockSpec((3, length, dil * LANES), lambda b: (0, b, 0)))
        out_specs += [pl.BlockSpec((length, dil * LANES), lambda b: (b, 0))] * 2
        out_shape += [jax.ShapeDtypeStruct((batch * length, dil * LANES), F32)] * 2
    return pl.pallas_call(
        functools.partial(_dilated_kernel, seq=seq),
        grid=(batch,),
        in_specs=in_specs,
        out_specs=out_specs,
        out_shape=out_shape,
        compiler_params=_params("parallel"),
        name="dilated_attention",
    )(*a_groups)


def _diff_kernel(lam_ref, q_ref, k_ref, v_ref, g_ref, o_ref, v1_ref, *, lam_init, seq):
    lam_p = lam_ref[...]
    lam = (jnp.exp(jnp.sum(lam_p[0:1] * lam_p[1:2], axis=-1, keepdims=True))
           - jnp.exp(jnp.sum(lam_p[2:3] * lam_p[3:4], axis=-1, keepdims=True)) + lam_init)
    lane = lax.broadcasted_iota(jnp.int32, (1, LANES), 1)
    first = lane < HEAD_DIM
    k = k_ref[0]
    gain = g_ref[...] * (1.0 - lam_init)

    v1_ref[:, :LANES] = v_ref[0]
    v1_ref[:, LANES:] = jnp.ones((seq, LANES), BF16)

    def score(unit):
        i, second_map = divmod(unit, 2)
        q = q_ref[0, i * B_TQ:(i + 1) * B_TQ, :]
        zero = jnp.zeros_like(q)
        qm = jnp.where(first, zero, q) if second_map else jnp.where(first, q, zero)
        return lax.dot_general(qm, k, (((1,), (1,)), ((), ())), preferred_element_type=F32)

    def softmax_pv(s):
        m = jnp.max(s, axis=-1, keepdims=True)
        e = jnp.exp2(s - m).astype(BF16)
        pv = jnp.dot(e, v1_ref[...], preferred_element_type=F32)
        return pv[:, :LANES] / pv[:, LANES:]

    n_units = 2 * (seq // B_TQ)
    pending = [score(u) for u in range(B_QK_AHEAD)]
    outs = []
    for u in range(n_units):
        cur = pending.pop(0)
        if u + B_QK_AHEAD < n_units:
            pending.append(score(u + B_QK_AHEAD))
        outs.append(softmax_pv(cur))
        if u % 2 == 1:
            rows = slice((u // 2) * B_TQ, (u // 2 + 1) * B_TQ)
            o = outs[-2] - lam * outs[-1]
            ms = jnp.mean(o * o, axis=-1, keepdims=True)
            o_ref[0, rows, :] = (o * lax.rsqrt(ms + NORM_EPS) * gain).astype(o_ref.dtype)


def _diff_attention(main3, lam_p, out_g, lam_init, batch, seq):
    qb, kb, vb = ((off - N_A) // LANES for off in (OFF_BQ, OFF_BK, OFF_BV))

    def head_block(col_block):
        return pl.BlockSpec((1, seq, LANES), lambda b, h: (b, 0, col_block + h))

    return pl.pallas_call(
        functools.partial(_diff_kernel, lam_init=lam_init, seq=seq),
        grid=(batch, B_HEADS),
        in_specs=[
            pl.BlockSpec((4, HEAD_DIM), lambda b, h: (0, 0)),
            head_block(qb), head_block(kb), head_block(vb),
            pl.BlockSpec((1, LANES), lambda b, h: (0, 0)),
        ],
        out_specs=head_block(0),
        out_shape=jax.ShapeDtypeStruct((batch, seq, B_WIDTH), BF16),
        scratch_shapes=[pltpu.VMEM((seq, 2 * LANES), BF16)],
        compiler_params=_params("parallel", "parallel"),
        name="diff_attention",
    )(lam_p, main3, main3, main3, out_g)


C_HEADS_PER_STEP = LANES // HEAD_DIM


def _retention_kernel(dec_ref, q_ref, k_ref, v_ref, cg_ref, g_ref, o_ref, *, seq):
    cc = RET_CHUNK
    n = seq // cc
    lane = lax.broadcasted_iota(jnp.int32, (1, LANES), 1)
    pos = lax.broadcasted_iota(jnp.int32, (cc, LANES), 0).astype(F32)
    rel = (lax.broadcasted_iota(jnp.int32, (cc, cc), 0) - lax.broadcasted_iota(jnp.int32, (cc, cc), 1)).astype(F32)
    g = g_ref[...]

    def rows(ref, i, cols=slice(None)):
        return ref[0, i * cc:(i + 1) * cc, cols]

    def head(hh):
        mine = (lane // HEAD_DIM) == hh
        cols = slice(hh * LANES, (hh + 1) * LANES)
        dec = dec_ref[hh]
        lg_f = -jnp.exp(dec[0:1, 0:1])
        lg_b = -jnp.exp(dec[1:2, 0:1])
        dq_f = jnp.exp(lg_f * (pos + 1.0)).astype(BF16)
        dk_f = jnp.exp(lg_f * (cc - 1.0 - pos)).astype(BF16)
        dq_b = jnp.exp(lg_b * (cc - pos)).astype(BF16)
        dk_b = jnp.exp(lg_b * pos).astype(BF16)
        chunk_f = jnp.exp(lg_f * cc)
        chunk_b = jnp.exp(lg_b * cc)
        decay = (jnp.where(rel >= 0, jnp.exp(lg_f * jnp.maximum(rel, 0.0)), 0.0)
                 + jnp.where(rel <= 0, jnp.exp(lg_b * jnp.maximum(-rel, 0.0)), 0.0))

        def kv_sum(i, dk):
            kd = rows(k_ref, i) * dk
            return lax.dot_general(kd, rows(v_ref, i, cols), (((0,), (0,)), ((), ())), preferred_element_type=F32)

        state = jnp.zeros((LANES, LANES), F32)
        fwd_states = []
        for i in range(n):
            fwd_states.append(state)
            if i + 1 < n:
                state = chunk_f * state + kv_sum(i, dk_f)
        state = jnp.zeros((LANES, LANES), F32)
        bwd_states = [None] * n
        for i in reversed(range(n)):
            bwd_states[i] = state
            if i > 0:
                state = chunk_b * state + kv_sum(i, dk_b)

        def qk(i):
            q = rows(q_ref, i)
            q = jnp.where(mine, q, jnp.zeros_like(q))
            return q, lax.dot_general(q, rows(k_ref, i), (((1,), (1,)), ((), ())), preferred_element_type=F32)

        def finish(i, q, scores):
            inner = jnp.dot((scores * decay).astype(BF16), rows(v_ref, i, cols), preferred_element_type=F32)
            q_cross = jnp.concatenate([q * dq_f, q * dq_b], axis=1)
            s_cross = jnp.concatenate([fwd_states[i], bwd_states[i]], axis=0).astype(BF16)
            out = inner + jnp.dot(q_cross, s_cross, preferred_element_type=F32)
            ms = jnp.mean(out * out, axis=-1, keepdims=True)
            y = out * lax.rsqrt(ms + NORM_EPS) * g
            o_ref[0, i * cc:(i + 1) * cc, cols] = (rows(cg_ref, i, cols).astype(F32) * y).astype(o_ref.dtype)

        return qk, finish

    heads = [head(hh) for hh in range(C_HEADS_PER_STEP)]
    units = [(hh, i) for i in range(n) for hh in range(C_HEADS_PER_STEP)]
    nxt = heads[units[0][0]][0](units[0][1])
    for u, (hh, i) in enumerate(units):
        q, scores = nxt
        if u + 1 < len(units):
            nxt = heads[units[u + 1][0]][0](units[u + 1][1])
        heads[hh][1](i, q, scores)


def _retention(main3, dec, out_g, batch, seq):
    hps = C_HEADS_PER_STEP
    qb, kb = ((off - N_A) // LANES for off in (OFF_CQ, OFF_CK))
    vb, gb = ((off - N_A) // (hps * LANES) for off in (OFF_CV, OFF_CG))
    return pl.pallas_call(
        functools.partial(_retention_kernel, seq=seq),
        grid=(batch, C_HEADS // hps),
        in_specs=[
            pl.BlockSpec((hps, 2, LANES), lambda b, j: (j, 0, 0)),
            pl.BlockSpec((1, seq, LANES), lambda b, j: (b, 0, qb + j)),
            pl.BlockSpec((1, seq, LANES), lambda b, j: (b, 0, kb + j)),
            pl.BlockSpec((1, seq, hps * LANES), lambda b, j: (b, 0, vb + j)),
            pl.BlockSpec((1, seq, hps * LANES), lambda b, j: (b, 0, gb + j)),
            pl.BlockSpec((1, LANES), lambda b, j: (0, 0)),
        ],
        out_specs=pl.BlockSpec((1, seq, hps * LANES), lambda b, j: (b, 0, j)),
        out_shape=jax.ShapeDtypeStruct((batch, seq, C_WIDTH), BF16),
        compiler_params=_params("parallel", "parallel"),
        name="retention",
    )(dec, main3, main3, main3, main3, out_g)


def _merge_mlp_kernel(x_ref, o0_ref, l0_ref, o1_ref, l1_ref, o2_ref, l2_ref, yb_ref, yc_ref, gate_ref,
                      wa_ref, wb_ref, wc_ref, wo_ref, g2_ref, w1_ref, w2_ref, o_ref, pos_ref):
    tm = x_ref.shape[0]
    os_, ls = [o0_ref[...]], [l0_ref[...]]
    slot = 0
    for (_, dil), o_in, l_in in zip(A_GROUPS[1:], (o1_ref, o2_ref), (l1_ref, l2_ref)):
        for src, dst in ((o_in, os_), (l_in, ls)):
            for r in range(dil):
                pos_ref[slot, pl.ds(r, tm // dil, stride=dil), :] = src[:, r * LANES:(r + 1) * LANES]
            dst.append(pos_ref[slot])
            slot += 1
    m = functools.reduce(jnp.maximum, ls)
    es = [jnp.exp2(l - m) for l in ls]
    inv = 1.0 / functools.reduce(jnp.add, es)
    ya = jnp.concatenate([(o * (e * inv)).astype(BF16) for o, e in zip(os_, es)], axis=1)

    merged = gate_ref[:, 0:D_MODEL].astype(F32) * jnp.dot(ya, wa_ref[...], preferred_element_type=F32)
    merged = merged + gate_ref[:, D_MODEL:2 * D_MODEL].astype(F32) * jnp.dot(
        yb_ref[...], wb_ref[...], preferred_element_type=F32)
    merged = merged + gate_ref[:, 2 * D_MODEL:3 * D_MODEL].astype(F32) * jnp.dot(
        yc_ref[...], wc_ref[...], preferred_element_type=F32)
    x1 = x_ref[...] + jnp.dot(merged.astype(BF16), wo_ref[...], preferred_element_type=F32)

    ms = jnp.mean(x1 * x1, axis=-1, keepdims=True)
    xn = (x1 * lax.rsqrt(ms + NORM_EPS) * g2_ref[...]).astype(BF16)
    acc = x1
    for c in range(D_FF // D_MODEL):
        lo, hi = c * D_MODEL, (c + 1) * D_MODEL
        hdn = jnp.dot(xn, w1_ref[:, lo:hi].astype(BF16), preferred_element_type=F32)
        hdn = jnp.square(jnp.maximum(hdn, 0.0)).astype(BF16)
        acc = acc + jnp.dot(hdn, w2_ref[lo:hi, :], preferred_element_type=F32)
    o_ref[...] = acc


def _merge_mlp(x2, ya_parts, yb, yc, gates, wa, wb, wc, wo, g2, w1, w2, layer):
    rows = x2.shape[0]
    tm = OUT_TM

    def row_block(width):
        return pl.BlockSpec((tm, width), lambda i: (i, 0))

    a_specs = []
    for _, dil in A_GROUPS:
        a_specs += [pl.BlockSpec((tm // dil, dil * LANES), lambda i: (i, 0))] * 2

    return pl.pallas_call(
        _merge_mlp_kernel,
        grid=(rows // tm,),
        in_specs=[row_block(D_MODEL)] + a_specs + [
            row_block(B_WIDTH), row_block(C_WIDTH), row_block(N_GATE),
            _layer_resident(wa, layer), _layer_resident(wb, layer), _layer_resident(wc, layer),
            _layer_resident(wo, layer), _resident(g2.shape), _layer_resident(w1, layer), _layer_resident(w2, layer)],
        out_specs=row_block(D_MODEL),
        out_shape=jax.ShapeDtypeStruct((rows, D_MODEL), F32),
        scratch_shapes=[pltpu.VMEM((2 * (len(A_GROUPS) - 1), tm, LANES), F32)],
        compiler_params=_params("parallel"),
        name="merge_mlp",
    )(x2, *ya_parts, yb, yc, gates, wa, wb, wc, wo, g2, w1, w2)


def kernel(x, norm1_g, w_in, a_q_norm_g, a_k_norm_g, b_q_norm_g, b_k_norm_g, b_lambda_q1, b_lambda_k1, b_lambda_q2, b_lambda_k2, b_out_norm_g, c_decay_f, c_decay_b, c_out_norm_g, w_br_a, w_br_b, w_br_c, w_o, norm2_g, w_mlp1, w_mlp2):
    batch, seq, _ = x.shape
    depth = w_in.shape[0]
    rope_p = _rope_tables(seq, PART_ROT, ROPE_THETA, HEAD_DIM)
    rope_r = _rope_tables(seq, HEAD_DIM, RET_THETA, HEAD_DIM)
    head_id = np.arange(MXU_DIM) // HEAD_DIM
    bd = jnp.asarray(head_id[:, None] == head_id[None, :], BF16)

    w_br_a, w_br_b, w_br_c, w_o, w_mlp2 = (w.astype(BF16) for w in (w_br_a, w_br_b, w_br_c, w_o, w_mlp2))

    q_scale = HEAD_DIM ** -0.5 * math.log2(math.e)
    x2 = x.reshape(batch * seq, D_MODEL)
    for l in range(depth):
        a_heads = A_WIDTH // HEAD_DIM
        b_heads2 = B_WIDTH // HEAD_DIM
        head_gain = jnp.concatenate([
            jnp.tile(a_q_norm_g[l] * q_scale, a_heads), jnp.tile(a_k_norm_g[l], a_heads),
            jnp.tile(b_q_norm_g[l] * q_scale, b_heads2),
            jnp.tile(b_k_norm_g[l], b_heads2)])[None, :]
        a0, a1, a2, main, gates = _in_proj(
            x2, norm1_g[l][None, :], w_in, l, bd, head_gain, rope_p, rope_r, seq)
        main3 = main.reshape(batch, seq, N_MAIN)

        ya_parts = _dilated_attention((a0, a1, a2), batch, seq)

        lam_init = 0.8 - 0.6 * math.exp(-0.3 * l)
        lam_p = jnp.stack([b_lambda_q1[l], b_lambda_k1[l], b_lambda_q2[l], b_lambda_k2[l]], axis=0)
        yb = _diff_attention(main3, lam_p, b_out_norm_g[l][None, :], lam_init, batch, seq)

        dec = jnp.broadcast_to(jnp.stack([c_decay_f[l], c_decay_b[l]], axis=1)[:, :, None], (C_HEADS, 2, LANES))
        yc = _retention(main3, dec, c_out_norm_g[l][None, :], batch, seq)

        x2 = _merge_mlp(
            x2, ya_parts, yb.reshape(batch * seq, B_WIDTH), yc.reshape(batch * seq, C_WIDTH), gates,
            w_br_a, w_br_b, w_br_c, w_o, norm2_g[l][None, :], w_mlp1, w_mlp2, l)
    return x2.reshape(batch, seq, D_MODEL)
```

```python
import functools
import math

import jax
import jax.numpy as jnp
import numpy as np
from jax import lax
from jax.experimental import pallas as pl
from jax.experimental.pallas import tpu as pltpu

F32 = jnp.float32
BF16 = jnp.bfloat16

LANES = 128
MXU_DIM = 256
VMEM_LIMIT_BYTES = 56 * 1024 * 1024

D_MODEL = 1024
HEAD_DIM = 64
A_GROUPS = ((128, 1), (512, 4), (2048, 16))
A_WIDTH = 384
B_HEADS = 4
B_WIDTH = 512
C_HEADS = 4
C_QK_WIDTH = 256
C_WIDTH = 512
D_FF = 4096
N_BRANCHES = 3
ROPE_THETA = 500000.0
RET_THETA = 10000.0
PART_ROT = 16
NORM_EPS = 1e-6
NEG_INF = -1e30

OFF_AQ, OFF_AK, OFF_AV = 0, 384, 768
OFF_BQ, OFF_BK, OFF_BV = 1152, 1664, 2176
OFF_CQ, OFF_CK, OFF_CV, OFF_CG = 2688, 2944, 3200, 3712
OFF_GATE = 4224
N_A = OFF_BQ
N_MAIN = OFF_GATE - N_A
N_GATE = N_BRANCHES * D_MODEL
N_IN = OFF_GATE + N_GATE

IN_TM = 512
OUT_TM = 512
B_TQ = 256
B_QK_AHEAD = 3
RET_CHUNK = 256
A_TQ = 128
A_HALF = 64
assert all(window // (2 * dilation) == A_HALF for window, dilation in A_GROUPS)


def _params(*sem):
    return pltpu.CompilerParams(dimension_semantics=sem, vmem_limit_bytes=VMEM_LIMIT_BYTES)


def _resident(shape):
    nd = len(shape)
    return pl.BlockSpec(shape, lambda *_: (0,) * nd, pipeline_mode=pl.Buffered(1))


def _layer_resident(stacked, layer):
    _, rows, cols = stacked.shape
    return pl.BlockSpec((None, rows, cols), lambda *_: (layer, 0, 0), pipeline_mode=pl.Buffered(1))


def _rope_tables(seq, rot, theta, head_dim):
    half = rot // 2
    inv = 1.0 / (theta ** (jnp.arange(0, rot, 2, dtype=F32) / rot))
    ang = jnp.arange(seq, dtype=F32)[:, None] * inv[None, :]
    cos, sin = jnp.cos(ang), jnp.sin(ang)
    pad = head_dim - rot
    ones = jnp.ones((seq, pad), F32)
    zeros = jnp.zeros((seq, pad), F32)
    zh = jnp.zeros((seq, half), F32)
    c = jnp.concatenate([cos, cos, ones], axis=1)
    s_up = jnp.concatenate([zh, sin, zeros], axis=1)
    s_dn = jnp.concatenate([-sin, zh, zeros], axis=1)
    reps = LANES // head_dim
    return jnp.stack([jnp.tile(t, (1, reps)) for t in (c, s_up, s_dn)], axis=0)


def _apply_rope(t, tab, half):
    return (t * tab[0]
            + pltpu.roll(t, half, 1) * tab[1]
            + pltpu.roll(t, LANES - half, 1) * tab[2])


def _in_proj_kernel(x_ref, g_ref, w_ref, bd_ref, hg_ref, rp_ref, rr_ref,
                    a0_ref, a1_ref, a2_ref, main_ref, gate_ref, stage_ref):
    a_refs = (a0_ref, a1_ref, a2_ref)
    n_groups = len(A_GROUPS)
    tm = x_ref.shape[0]
    x = x_ref[...]
    xg = (x * g_ref[...]).astype(BF16)
    row_scale = lax.rsqrt(jnp.mean(x * x, axis=-1, keepdims=True) + NORM_EPS)
    bd = bd_ref[...]
    rp = rp_ref[...]
    rr = rr_ref[...]

    def proj(lo, hi):
        return jnp.dot(xg, w_ref[:, lo:hi].astype(BF16), preferred_element_type=F32) * row_scale

    def store_a(slab, t):
        part, g = divmod(slab, n_groups)
        dil = A_GROUPS[g][1]
        if dil == 1:
            a_refs[g][part] = t.astype(BF16)
            return
        stage = stage_ref.at[part * (n_groups - 1) + g - 1]
        stage[...] = t
        for r in range(dil):
            a_refs[g][part, :, r * LANES:(r + 1) * LANES] = stage[pl.ds(r, tm // dil, stride=dil), :].astype(BF16)

    def store(lo, t):
        hi = lo + t.shape[1]
        if hi <= N_A:
            for c in range(t.shape[1] // LANES):
                store_a(lo // LANES + c, t[:, c * LANES:(c + 1) * LANES])
        else:
            main_ref[:, lo - N_A:hi - N_A] = t.astype(BF16)

    def store_roped(t, lo, tab, half, scale=None):
        for c in range(t.shape[1] // LANES):
            o = _apply_rope(t[:, c * LANES:(c + 1) * LANES], tab, half)
            if scale is not None:
                o = o * scale
            store(lo + c * LANES, o)

    hg_off = 0
    for seg_lo, seg_hi in ((OFF_AQ, OFF_AV), (OFF_BQ, OFF_BV)):
        seg = proj(seg_lo, seg_hi)
        for lo in range(0, seg_hi - seg_lo, MXU_DIM):
            t = seg[:, lo:lo + MXU_DIM]
            ss = jnp.dot((t * t).astype(BF16), bd, preferred_element_type=F32)
            t = t * lax.rsqrt(ss * (1.0 / HEAD_DIM) + NORM_EPS) * hg_ref[:, hg_off:hg_off + MXU_DIM]
            store_roped(t, seg_lo + lo, rp, PART_ROT // 2)
            hg_off += MXU_DIM

    seg = proj(OFF_CQ, OFF_CV)
    store_roped(seg[:, :C_QK_WIDTH], OFF_CQ, rr, HEAD_DIM // 2)
    store_roped(seg[:, C_QK_WIDTH:], OFF_CK, rr, HEAD_DIM // 2, scale=HEAD_DIM ** -0.5)

    for lo, hi in ((OFF_AV, OFF_BQ), (OFF_BV, OFF_CQ), (OFF_CV, OFF_CG)):
        store(lo, proj(lo, hi))
    store(OFF_CG, jax.nn.silu(proj(OFF_CG, OFF_GATE)))

    for j in range(N_BRANCHES):
        lo = OFF_GATE + j * D_MODEL
        gate_ref[:, j * D_MODEL:(j + 1) * D_MODEL] = jax.nn.sigmoid(proj(lo, lo + D_MODEL)).astype(BF16)


def _in_proj(x2, g, w_stack, layer, bd, hg, rope_p, rope_r, seq):
    rows = x2.shape[0]
    tm = IN_TM
    per_seq = seq // tm
    return pl.pallas_call(
        _in_proj_kernel,
        grid=(rows // tm,),
        in_specs=[
            pl.BlockSpec((tm, D_MODEL), lambda i: (i, 0)),
            _resident((1, D_MODEL)),
            _layer_resident(w_stack, layer),
            _resident((MXU_DIM, MXU_DIM)),
            _resident((1, hg.shape[1])),
            pl.BlockSpec((3, tm, LANES), lambda i: (0, i % per_seq, 0)),
            pl.BlockSpec((3, tm, LANES), lambda i: (0, i % per_seq, 0)),
        ],
        out_specs=[
            pl.BlockSpec((3, tm // dil, dil * LANES), lambda i: (0, i, 0)) for _, dil in A_GROUPS
        ] + [
            pl.BlockSpec((tm, N_MAIN), lambda i: (i, 0)),
            pl.BlockSpec((tm, N_GATE), lambda i: (i, 0)),
        ],
        out_shape=[
            jax.ShapeDtypeStruct((3, rows // dil, dil * LANES), BF16) for _, dil in A_GROUPS
        ] + [
            jax.ShapeDtypeStruct((rows, N_MAIN), BF16),
            jax.ShapeDtypeStruct((rows, N_GATE), BF16),
        ],
        scratch_shapes=[pltpu.VMEM((3 * (len(A_GROUPS) - 1), tm, LANES), F32)],
        compiler_params=_params("parallel"),
        name="in_proj",
    )(x2, g, w_stack, bd, hg, rope_p, rope_r)


def _dilated_kernel(a0_ref, a1_ref, a2_ref, o0_ref, l0_ref, o1_ref, l1_ref, o2_ref, l2_ref, *, seq):
    lane = lax.broadcasted_iota(jnp.int32, (1, LANES), 1)
    first = lane < HEAD_DIM
    for (_, dil), a_ref, o_ref, l_ref in zip(A_GROUPS, (a0_ref, a1_ref, a2_ref), (o0_ref, o1_ref, o2_ref),
                                             (l0_ref, l1_ref, l2_ref)):
        length = seq // dil
        tq = min(A_TQ, length)
        kw = min(tq + 2 * A_HALF, length)
        rel = (lax.broadcasted_iota(jnp.int32, (tq, kw), 0) - lax.broadcasted_iota(jnp.int32, (tq, kw), 1))
        ones = jnp.ones((kw, LANES), BF16)
        masks = {}
        for r in range(dil):
            lanes = slice(r * LANES, (r + 1) * LANES)
            for n0 in range(0, length, tq):
                k0 = min(max(n0 - A_HALF, 0), length - kw)
                if n0 - k0 not in masks:
                    masks[n0 - k0] = jnp.abs(rel + (n0 - k0)) <= A_HALF
                valid = masks[n0 - k0]
                q = a_ref[0, n0:n0 + tq, lanes]
                k = a_ref[1, k0:k0 + kw, lanes]
                v1 = jnp.concatenate([a_ref[2, k0:k0 + kw, lanes], ones], axis=1)
                outs, lses = [], []
                for head_lanes in (first, jnp.logical_not(first)):
                    qh = jnp.where(head_lanes, q, jnp.zeros_like(q))
                    s = lax.dot_general(qh, k, (((1,), (1,)), ((), ())), preferred_element_type=F32)
                    s = jnp.where(valid, s, NEG_INF)
                    m = jnp.max(s, axis=-1, keepdims=True)
                    e = jnp.exp2(s - m).astype(BF16)
                    pv = jnp.dot(e, v1, preferred_element_type=F32)
                    den = pv[:, LANES:]
                    outs.append(pv[:, :LANES] / den)
                    lses.append(m + jnp.log2(den))
                o_ref[n0:n0 + tq, lanes] = jnp.where(first, outs[0], outs[1])
                l_ref[n0:n0 + tq, lanes] = jnp.where(first, lses[0], lses[1])


def _dilated_attention(a_groups, batch, seq):
    in_specs, out_specs, out_shape = [], [], []
    for _, dil in A_GROUPS:
        length = seq // dil
        in_specs.append(pl.BlockSpec((3, length, dil * LANES), lambda b: (0, b, 0)))
        out_specs += [pl.BlockSpec((length, dil * LANES), lambda b: (b, 0))] * 2
        out_shape += [jax.ShapeDtypeStruct((batch * length, dil * LANES), F32)] * 2
    return pl.pallas_call(
        functools.partial(_dilated_kernel, seq=seq),
        grid=(batch,),
        in_specs=in_specs,
        out_specs=out_specs,
        out_shape=out_shape,
        compiler_params=_params("parallel"),
        name="dilated_attention",
    )(*a_groups)


def _diff_kernel(lam_ref, q_ref, k_ref, v_ref, g_ref, o_ref, v1_ref, *, lam_init, seq):
    lam_p = lam_ref[...]
    lam = (jnp.exp(jnp.sum(lam_p[0:1] * lam_p[1:2], axis=-1, keepdims=True))
           - jnp.exp(jnp.sum(lam_p[2:3] * lam_p[3:4], axis=-1, keepdims=True)) + lam_init)
    lane = lax.broadcasted_iota(jnp.int32, (1, LANES), 1)
    first = lane < HEAD_DIM
    k = k_ref[0]
    gain = g_ref[...] * (1.0 - lam_init)

    v1_ref[:, :LANES] = v_ref[0]
    v1_ref[:, LANES:] = jnp.ones((seq, LANES), BF16)

    def score(unit):
        i, second_map = divmod(unit, 2)
        q = q_ref[0, i * B_TQ:(i + 1) * B_TQ, :]
        zero = jnp.zeros_like(q)
        qm = jnp.where(first, zero, q) if second_map else jnp.where(first, q, zero)
        return lax.dot_general(qm, k, (((1,), (1,)), ((), ())), preferred_element_type=F32)

    def softmax_pv(s):
        m = jnp.max(s, axis=-1, keepdims=True)
        e = jnp.exp2(s - m).astype(BF16)
        pv = jnp.dot(e, v1_ref[...], preferred_element_type=F32)
        return pv[:, :LANES] / pv[:, LANES:]

    n_units = 2 * (seq // B_TQ)
    pending = [score(u) for u in range(B_QK_AHEAD)]
    outs = []
    for u in range(n_units):
        cur = pending.pop(0)
        if u + B_QK_AHEAD < n_units:
            pending.append(score(u + B_QK_AHEAD))
        outs.append(softmax_pv(cur))
        if u % 2 == 1:
            rows = slice((u // 2) * B_TQ, (u // 2 + 1) * B_TQ)
            o = outs[-2] - lam * outs[-1]
            ms = jnp.mean(o * o, axis=-1, keepdims=True)
            o_ref[0, rows, :] = (o * lax.rsqrt(ms + NORM_EPS) * gain).astype(o_ref.dtype)


def _diff_attention(main3, lam_p, out_g, lam_init, batch, seq):
    qb, kb, vb = ((off - N_A) // LANES for off in (OFF_BQ, OFF_BK, OFF_BV))

    def head_block(col_block):
        return pl.BlockSpec((1, seq, LANES), lambda b, h: (b, 0, col_block + h))

    return pl.pallas_call(
        functools.partial(_diff_kernel, lam_init=lam_init, seq=seq),
        grid=(batch, B_HEADS),
        in_specs=[
            pl.BlockSpec((4, HEAD_DIM), lambda b, h: (0, 0)),
            head_block(qb), head_block(kb), head_block(vb),
            pl.BlockSpec((1, LANES), lambda b, h: (0, 0)),
        ],
        out_specs=head_block(0),
        out_shape=jax.ShapeDtypeStruct((batch, seq, B_WIDTH), BF16),
        scratch_shapes=[pltpu.VMEM((seq, 2 * LANES), BF16)],
        compiler_params=_params("parallel", "parallel"),
        name="diff_attention",
    )(lam_p, main3, main3, main3, out_g)


C_HEADS_PER_STEP = LANES // HEAD_DIM


def _retention_kernel(dec_ref, q_ref, k_ref, v_ref, cg_ref, g_ref, o_ref, *, seq):
    cc = RET_CHUNK
    n = seq // cc
    lane = lax.broadcasted_iota(jnp.int32, (1, LANES), 1)
    pos = lax.broadcasted_iota(jnp.int32, (cc, LANES), 0).astype(F32)
    rel = (lax.broadcasted_iota(jnp.int32, (cc, cc), 0) - lax.broadcasted_iota(jnp.int32, (cc, cc), 1)).astype(F32)
    g = g_ref[...]

    def rows(ref, i, cols=slice(None)):
        return ref[0, i * cc:(i + 1) * cc, cols]

    def head(hh):
        mine = (lane // HEAD_DIM) == hh
        cols = slice(hh * LANES, (hh + 1) * LANES)
        dec = dec_ref[hh]
        lg_f = -jnp.exp(dec[0:1, 0:1])
        lg_b = -jnp.exp(dec[1:2, 0:1])
        dq_f = jnp.exp(lg_f * (pos + 1.0)).astype(BF16)
        dk_f = jnp.exp(lg_f * (cc - 1.0 - pos)).astype(BF16)
        dq_b = jnp.exp(lg_b * (cc - pos)).astype(BF16)
        dk_b = jnp.exp(lg_b * pos).astype(BF16)
        chunk_f = jnp.exp(lg_f * cc)
        chunk_b = jnp.exp(lg_b * cc)
        decay = (jnp.where(rel >= 0, jnp.exp(lg_f * jnp.maximum(rel, 0.0)), 0.0)
                 + jnp.where(rel <= 0, jnp.exp(lg_b * jnp.maximum(-rel, 0.0)), 0.0))

        def kv_sum(i, dk):
            kd = rows(k_ref, i) * dk
            return lax.dot_general(kd, rows(v_ref, i, cols), (((0,), (0,)), ((), ())), preferred_element_type=F32)

        state = jnp.zeros((LANES, LANES), F32)
        fwd_states = []
        for i in range(n):
            fwd_states.append(state)
            if i + 1 < n:
                state = chunk_f * state + kv_sum(i, dk_f)
        state = jnp.zeros((LANES, LANES), F32)
        bwd_states = [None] * n
        for i in reversed(range(n)):
            bwd_states[i] = state
            if i > 0:
                state = chunk_b * state + kv_sum(i, dk_b)

        def qk(i):
            q = rows(q_ref, i)
            q = jnp.where(mine, q, jnp.zeros_like(q))
            return q, lax.dot_general(q, rows(k_ref, i), (((1,), (1,)), ((), ())), preferred_element_type=F32)

        def finish(i, q, scores):
            inner = jnp.dot((scores * decay).astype(BF16), rows(v_ref, i, cols), preferred_element_type=F32)
            q_cross = jnp.concatenate([q * dq_f, q * dq_b], axis=1)
            s_cross = jnp.concatenate([fwd_states[i], bwd_states[i]], axis=0).astype(BF16)
            out = inner + jnp.dot(q_cross, s_cross, preferred_element_type=F32)
            ms = jnp.mean(out * out, axis=-1, keepdims=True)
            y = out * lax.rsqrt(ms + NORM_EPS) * g
            o_ref[0, i * cc:(i + 1) * cc, cols] = (rows(cg_ref, i, cols).astype(F32) * y).astype(o_ref.dtype)

        return qk, finish

    heads = [head(hh) for hh in range(C_HEADS_PER_STEP)]
    units = [(hh, i) for i in range(n) for hh in range(C_HEADS_PER_STEP)]
    nxt = heads[units[0][0]][0](units[0][1])
    for u, (hh, i) in enumerate(units):
        q, scores = nxt
        if u + 1 < len(units):
            nxt = heads[units[u + 1][0]][0](units[u + 1][1])
        heads[hh][1](i, q, scores)


def _retention(main3, dec, out_g, batch, seq):
    hps = C_HEADS_PER_STEP
    qb, kb = ((off - N_A) // LANES for off in (OFF_CQ, OFF_CK))
    vb, gb = ((off - N_A) // (hps * LANES) for off in (OFF_CV, OFF_CG))
    return pl.pallas_call(
        functools.partial(_retention_kernel, seq=seq),
        grid=(batch, C_HEADS // hps),
        in_specs=[
            pl.BlockSpec((hps, 2, LANES), lambda b, j: (j, 0, 0)),
            pl.BlockSpec((1, seq, LANES), lambda b, j: (b, 0, qb + j)),
            pl.BlockSpec((1, seq, LANES), lambda b, j: (b, 0, kb + j)),
            pl.BlockSpec((1, seq, hps * LANES), lambda b, j: (b, 0, vb + j)),
            pl.BlockSpec((1, seq, hps * LANES), lambda b, j: (b, 0, gb + j)),
            pl.BlockSpec((1, LANES), lambda b, j: (0, 0)),
        ],
        out_specs=pl.BlockSpec((1, seq, hps * LANES), lambda b, j: (b, 0, j)),
        out_shape=jax.ShapeDtypeStruct((batch, seq, C_WIDTH), BF16),
        compiler_params=_params("parallel", "parallel"),
        name="retention",
    )(dec, main3, main3, main3, main3, out_g)


def _merge_mlp_kernel(x_ref, o0_ref, l0_ref, o1_ref, l1_ref, o2_ref, l2_ref, yb_ref, yc_ref, gate_ref,
                      wa_ref, wb_ref, wc_ref, wo_ref, g2_ref, w1_ref, w2_ref, o_ref, pos_ref):
    tm = x_ref.shape[0]
    os_, ls = [o0_ref[...]], [l0_ref[...]]
    slot = 0
    for (_, dil), o_in, l_in in zip(A_GROUPS[1:], (o1_ref, o2_ref), (l1_ref, l2_ref)):
        for src, dst in ((o_in, os_), (l_in, ls)):
            for r in range(dil):
                pos_ref[slot, pl.ds(r, tm // dil, stride=dil), :] = src[:, r * LANES:(r + 1) * LANES]
            dst.append(pos_ref[slot])
            slot += 1
    m = functools.reduce(jnp.maximum, ls)
    es = [jnp.exp2(l - m) for l in ls]
    inv = 1.0 / functools.reduce(jnp.add, es)
    ya = jnp.concatenate([(o * (e * inv)).astype(BF16) for o, e in zip(os_, es)], axis=1)

    merged = gate_ref[:, 0:D_MODEL].astype(F32) * jnp.dot(ya, wa_ref[...], preferred_element_type=F32)
    merged = merged + gate_ref[:, D_MODEL:2 * D_MODEL].astype(F32) * jnp.dot(
        yb_ref[...], wb_ref[...], preferred_element_type=F32)
    merged = merged + gate_ref[:, 2 * D_MODEL:3 * D_MODEL].astype(F32) * jnp.dot(
        yc_ref[...], wc_ref[...], preferred_element_type=F32)
    x1 = x_ref[...] + jnp.dot(merged.astype(BF16), wo_ref[...], preferred_element_type=F32)

    xg = (x1 * g2_ref[...]).astype(BF16)
    row_scale = lax.rsqrt(jnp.mean(x1 * x1, axis=-1, keepdims=True) + NORM_EPS)
    acc = x1
    for c in range(D_FF // D_MODEL):
        lo, hi = c * D_MODEL, (c + 1) * D_MODEL
        hdn = jnp.dot(xg, w1_ref[:, lo:hi].astype(BF16), preferred_element_type=F32) * row_scale
        hdn = jnp.square(jnp.maximum(hdn, 0.0)).astype(BF16)
        acc = acc + jnp.dot(hdn, w2_ref[lo:hi, :], preferred_element_type=F32)
    o_ref[...] = acc


def _merge_mlp(x2, ya_parts, yb, yc, gates, wa, wb, wc, wo, g2, w1, w2, layer):
    rows = x2.shape[0]
    tm = OUT_TM

    def row_block(width):
        return pl.BlockSpec((tm, width), lambda i: (i, 0))

    a_specs = []
    for _, dil in A_GROUPS:
        a_specs += [pl.BlockSpec((tm // dil, dil * LANES), lambda i: (i, 0))] * 2

    return pl.pallas_call(
        _merge_mlp_kernel,
        grid=(rows // tm,),
        in_specs=[row_block(D_MODEL)] + a_specs + [
            row_block(B_WIDTH), row_block(C_WIDTH), row_block(N_GATE),
            _layer_resident(wa, layer), _layer_resident(wb, layer), _layer_resident(wc, layer),
            _layer_resident(wo, layer), _resident(g2.shape), _layer_resident(w1, layer), _layer_resident(w2, layer)],
        out_specs=row_block(D_MODEL),
        out_shape=jax.ShapeDtypeStruct((rows, D_MODEL), F32),
        scratch_shapes=[pltpu.VMEM((2 * (len(A_GROUPS) - 1), tm, LANES), F32)],
        compiler_params=_params("parallel"),
        name="merge_mlp",
    )(x2, *ya_parts, yb, yc, gates, wa, wb, wc, wo, g2, w1, w2)


def kernel(x, norm1_g, w_in, a_q_norm_g, a_k_norm_g, b_q_norm_g, b_k_norm_g, b_lambda_q1, b_lambda_k1, b_lambda_q2, b_lambda_k2, b_out_norm_g, c_decay_f, c_decay_b, c_out_norm_g, w_br_a, w_br_b, w_br_c, w_o, norm2_g, w_mlp1, w_mlp2):
    batch, seq, _ = x.shape
    depth = w_in.shape[0]
    rope_p = _rope_tables(seq, PART_ROT, ROPE_THETA, HEAD_DIM)
    rope_r = _rope_tables(seq, HEAD_DIM, RET_THETA, HEAD_DIM)
    head_id = np.arange(MXU_DIM) // HEAD_DIM
    bd = jnp.asarray(head_id[:, None] == head_id[None, :], BF16)

    w_br_a, w_br_b, w_br_c, w_o, w_mlp2 = (w.astype(BF16) for w in (w_br_a, w_br_b, w_br_c, w_o, w_mlp2))

    q_scale = HEAD_DIM ** -0.5 * math.log2(math.e)
    x2 = x.reshape(batch * seq, D_MODEL)
    for l in range(depth):
        a_heads = A_WIDTH // HEAD_DIM
        b_heads2 = B_WIDTH // HEAD_DIM
        head_gain = jnp.concatenate([
            jnp.tile(a_q_norm_g[l] * q_scale, a_heads), jnp.tile(a_k_norm_g[l], a_heads),
            jnp.tile(b_q_norm_g[l] * q_scale, b_heads2),
            jnp.tile(b_k_norm_g[l], b_heads2)])[None, :]
        a0, a1, a2, main, gates = _in_proj(
            x2, norm1_g[l][None, :], w_in, l, bd, head_gain, rope_p, rope_r, seq)
        main3 = main.reshape(batch, seq, N_MAIN)

        ya_parts = _dilated_attention((a0, a1, a2), batch, seq)

        lam_init = 0.8 - 0.6 * math.exp(-0.3 * l)
        lam_p = jnp.stack([b_lambda_q1[l], b_lambda_k1[l], b_lambda_q2[l], b_lambda_k2[l]], axis=0)
        yb = _diff_attention(main3, lam_p, b_out_norm_g[l][None, :], lam_init, batch, seq)

        dec = jnp.broadcast_to(jnp.stack([c_decay_f[l], c_decay_b[l]], axis=1)[:, :, None], (C_HEADS, 2, LANES))
        yc = _retention(main3, dec, c_out_norm_g[l][None, :], batch, seq)

        x2 = _merge_mlp(
            x2, ya_parts, yb.reshape(batch * seq, B_WIDTH), yc.reshape(batch * seq, C_WIDTH), gates,
            w_br_a, w_br_b, w_br_c, w_o, norm2_g[l][None, :], w_mlp1, w_mlp2, l)
    return x2.reshape(batch, seq, D_MODEL)
```
